```python
import jax
import jax.numpy as jnp
from jax import lax
import numpy as np

D_MODEL = 1024
BATCH = 4
SEQ = 4096
DEPTH = 1

GRID_W = 64
CTX_LEN = 256
N_MOD = 6
H_ATTN = 8
KV_ATTN = 2
HD_ATTN = 64
GQA_GROUP = H_ATTN // KV_ATTN
ATTN_WIDTH = H_ATTN * HD_ATTN
KV_WIDTH = KV_ATTN * HD_ATTN
ROPE_THETA = 10000.0
ROPE_PAIRS_AXIS = HD_ATTN // 4
Q_BLOCK = 128
H_GDN = 4
DK_GDN = 128
DV_GDN = 128
GDN_QK = H_GDN * DK_GDN
GDN_QKV_WIDTH = H_GDN * (2 * DK_GDN + DV_GDN)
GDN_WIDTH = H_GDN * DV_GDN
GDN_CHUNK = 64
SHORT_CONV = 3
MIX_WIDTH = ATTN_WIDTH + GDN_WIDTH
IN_SPLITS = (ATTN_WIDTH, KV_WIDTH, KV_WIDTH, GDN_QKV_WIDTH, GDN_WIDTH, H_GDN, H_GDN, H_GDN, H_GDN)
IN_WIDTH = ATTN_WIDTH + 2 * KV_WIDTH + GDN_QKV_WIDTH + GDN_WIDTH + 4 * H_GDN
D_FF = 2816
FFN_CONV = 3
EPS = 1e-6

kernel_name = "hymba_style_attn_gdn_dit_block"


def rms_norm(x, g):
    xf = x.astype(jnp.float32)
    y = xf * lax.rsqrt(jnp.mean(xf * xf, axis=-1, keepdims=True) + EPS)
    return (y * g.astype(jnp.float32)).astype(x.dtype)


def l2_norm(x):
    return x * lax.rsqrt(jnp.sum(x * x, axis=-1, keepdims=True) + EPS)


def ada_terms(cvec, w_mod, b_mod):
    m = jax.nn.silu(cvec) @ w_mod + b_mod
    m = m.reshape(m.shape[:-1] + (N_MOD, D_MODEL))
    return [m[..., i, None, :] for i in range(N_MOD)]


def modulate(x, g, shift, scale):
    return rms_norm(x, g) * (1.0 + scale) + shift


def depthwise_conv(x, w):
    pad = w.shape[0] // 2
    return lax.conv_general_dilated(
        x, w[:, None, :].astype(x.dtype), window_strides=(1,), padding=((pad, pad),),
        dimension_numbers=("NWC", "WIO", "NWC"), feature_group_count=x.shape[-1])


def axial_rope_tables(n_tokens):
    rows = n_tokens // GRID_W
    row = jnp.repeat(jnp.arange(rows, dtype=jnp.float32), GRID_W)
    col = jnp.tile(jnp.arange(GRID_W, dtype=jnp.float32), rows)
    inv_freq = ROPE_THETA ** (-jnp.arange(ROPE_PAIRS_AXIS, dtype=jnp.float32) / ROPE_PAIRS_AXIS)
    ang = jnp.concatenate([row[:, None] * inv_freq, col[:, None] * inv_freq], axis=-1)
    return jnp.cos(ang), jnp.sin(ang)


def apply_rope(x, cos, sin):
    half = x.shape[-1] // 2
    x1, x2 = x[..., :half], x[..., half:]
    c, s = cos[None, :, None, :], sin[None, :, None, :]
    return jnp.concatenate([x1 * c - x2 * s, x2 * c + x1 * s], axis=-1)


def attn_q(q, q_norm_g, rope):
    B, T = q.shape[:2]
    q = rms_norm(q.reshape(B, T, H_ATTN, HD_ATTN), q_norm_g).astype(jnp.float32)
    if rope is not None:
        q = apply_rope(q, *rope)
    return q.reshape(B, T, KV_ATTN, GQA_GROUP, HD_ATTN) * (HD_ATTN ** -0.5)


def attn_kv(k, v, k_norm_g, rope):
    B, T = k.shape[:2]
    k = rms_norm(k.reshape(B, T, KV_ATTN, HD_ATTN), k_norm_g).astype(jnp.float32)
    if rope is not None:
        k = apply_rope(k, *rope)
    return k, v.reshape(B, T, KV_ATTN, HD_ATTN).astype(jnp.float32)


def block_attention(q, k, v):
    B, T = q.shape[:2]
    qb = jnp.swapaxes(q.reshape((B, T // Q_BLOCK, Q_BLOCK) + q.shape[2:]), 0, 1)

    def one_block(q_blk):
        s = jnp.einsum("bqkgd,bskd->bkgqs", q_blk, k)
        p = jax.nn.softmax(s, axis=-1)
        return jnp.einsum("bkgqs,bskd->bqkgd", p, v)

    o = lax.map(one_block, qb)
    return jnp.swapaxes(o, 0, 1).reshape(B, T, -1)


def gdn_inputs(qkv_raw, conv_w, with_q):
    if not with_q:
        qkv_raw, conv_w = qkv_raw[..., GDN_QK:], conv_w[:, GDN_QK:]
    y = jax.nn.silu(depthwise_conv(qkv_raw, conv_w).astype(jnp.float32))
    B, T = y.shape[:2]
    q = None
    if with_q:
        q = l2_norm(y[..., :GDN_QK].reshape(B, T, H_GDN, DK_GDN)) * (DK_GDN ** -0.5)
        y = y[..., GDN_QK:]
    k = l2_norm(y[..., :GDN_QK].reshape(B, T, H_GDN, DK_GDN))
    v = y[..., GDN_QK:].reshape(B, T, H_GDN, DV_GDN)
    return q, k, v


def gdn_decay(a, a_log, dt_bias):
    return -jnp.exp(a_log.astype(jnp.float32)) * jax.nn.softplus(a.astype(jnp.float32) + dt_bias.astype(jnp.float32))


def chunk_gated_delta(q, k, v, g, beta, s0):
    with_output = q is not None
    B, T, H, dk = k.shape
    dv = v.shape[-1]
    C = GDN_CHUNK
    N = T // C

    def to_chunks(a):
        a = jnp.moveaxis(a, 2, 1)
        return a.reshape((B, H, N, C) + a.shape[3:])

    k, v, g, beta = to_chunks(k), to_chunks(v), to_chunks(g), to_chunks(beta)
    g_cum = jnp.cumsum(g, axis=-1)
    g_last = g_cum[..., -1]
    idx = jnp.arange(C)
    incl = idx[:, None] >= idx[None, :]
    strict = idx[:, None] > idx[None, :]
    diff = g_cum[..., :, None] - g_cum[..., None, :]
    decay_incl = jnp.exp(jnp.where(incl, diff, -jnp.inf))
    decay_strict = jnp.where(strict, decay_incl, 0.0)
    k_beta = k * beta[..., None]
    lower = jnp.einsum("bhnid,bhnjd->bhnij", k_beta, k) * decay_strict
    eye = jnp.eye(C, dtype=jnp.float32)
    t_inv = lax.linalg.triangular_solve(lower + eye, jnp.broadcast_to(eye, lower.shape),
                                        left_side=True, lower=True, unit_diagonal=True)
    u = t_inv @ (v * beta[..., None])
    w = t_inv @ (k_beta * jnp.exp(g_cum)[..., None])
    k_dec = k * jnp.exp(g_last[..., None] - g_cum)[..., None]
    decay_last = jnp.exp(g_last)
    xs = (k_dec, w, u, decay_last)
    if with_output:
        q = to_chunks(q)
        q_dec = q * jnp.exp(g_cum)[..., None]
        intra = jnp.einsum("bhnid,bhnjd->bhnij", q, k) * decay_incl
        xs = xs + (q_dec, intra)
    xs = tuple(jnp.moveaxis(a, 2, 0) for a in xs)

    def step(state, xs_n):
        k_n, w_n, u_n, d_n = xs_n[:4]
        v_new = u_n - jnp.einsum("bhcd,bhdv->bhcv", w_n, state)
        new_state = state * d_n[..., None, None] + jnp.einsum("bhcd,bhcv->bhdv", k_n, v_new)
        if not with_output:
            return new_state, None
        q_n, a_n = xs_n[4:]
        o_n = jnp.einsum("bhcd,bhdv->bhcv", q_n, state) + jnp.einsum("bhij,bhjv->bhiv", a_n, v_new)
        return new_state, o_n

    state, o = lax.scan(step, s0, xs)
    if not with_output:
        return None, state
    o = jnp.moveaxis(o, 0, 2).reshape(B, H, T, dv)
    return jnp.moveaxis(o, 1, 2), state


def flip(a):
    return None if a is None else a[:, ::-1]


def gated_rms(o, z, g):
    B, T = z.shape[:2]
    zz = z.reshape(B, T, H_GDN, DV_GDN).astype(jnp.float32)
    return (rms_norm(o, g) * jax.nn.silu(zz)).reshape(B, T, GDN_WIDTH).astype(z.dtype)


def hybrid_mixer(h, hc, rope, w_in, q_norm_g, k_norm_g, attn_out_g, conv_qkv_w,
                 a_log_f, a_log_b, dt_bias_f, dt_bias_b, gdn_norm_g, w_out, with_ctx_out):
    B = h.shape[0]
    offs = np.cumsum(IN_SPLITS)[:-1].tolist()
    q_a, k_a, v_a, qkv_d, z_d, a_f, a_b, b_f, b_b = jnp.split(h @ w_in, offs, axis=-1)
    cq_a, ck_a, cv_a, cqkv_d, cz_d, ca_f, ca_b, cb_f, cb_b = jnp.split(hc @ w_in, offs, axis=-1)

    kc, vc = attn_kv(ck_a, cv_a, k_norm_g, None)
    kl, vl = attn_kv(k_a, v_a, k_norm_g, rope)
    ql = attn_q(q_a, q_norm_g, rope)
    o_attn = block_attention(ql, jnp.concatenate([kc, kl], axis=1), jnp.concatenate([vc, vl], axis=1))
    o_attn = rms_norm(o_attn.astype(h.dtype), attn_out_g)

    qd, kd, vd = gdn_inputs(qkv_d, conv_qkv_w, True)
    qdc, kdc, vdc = gdn_inputs(cqkv_d, conv_qkv_w, with_ctx_out)
    g_f, g_b = gdn_decay(a_f, a_log_f, dt_bias_f), gdn_decay(a_b, a_log_b, dt_bias_b)
    gc_f, gc_b = gdn_decay(ca_f, a_log_f, dt_bias_f), gdn_decay(ca_b, a_log_b, dt_bias_b)
    be_f, be_b = jax.nn.sigmoid(b_f.astype(jnp.float32)), jax.nn.sigmoid(b_b.astype(jnp.float32))
    bec_f, bec_b = jax.nn.sigmoid(cb_f.astype(jnp.float32)), jax.nn.sigmoid(cb_b.astype(jnp.float32))
    s0 = jnp.zeros((B, H_GDN, DK_GDN, DV_GDN), jnp.float32)
    oc_f, s_ctx_f = chunk_gated_delta(qdc, kdc, vdc, gc_f, bec_f, s0)
    oc_b, s_ctx_b = chunk_gated_delta(flip(qdc), flip(kdc), flip(vdc), flip(gc_b), flip(bec_b), s0)
    ol_f, _ = chunk_gated_delta(qd, kd, vd, g_f, be_f, s_ctx_f)
    ol_b, _ = chunk_gated_delta(flip(qd), flip(kd), flip(vd), flip(g_b), flip(be_b), s_ctx_b)
    o_gdn = gated_rms(ol_f + flip(ol_b), z_d, gdn_norm_g)

    lat_out = jnp.concatenate([o_attn, o_gdn], axis=-1) @ w_out
    if not with_ctx_out:
        return lat_out, None
    qc = attn_q(cq_a, q_norm_g, None)
    oc_attn = rms_norm(block_attention(qc, kc, vc).astype(hc.dtype), attn_out_g)
    oc_gdn = gated_rms(oc_f + flip(oc_b), cz_d, gdn_norm_g)
    ctx_out = jnp.concatenate([oc_attn, oc_gdn], axis=-1) @ w_out
    return lat_out, ctx_out


def conv_ffn(h, w_up, conv_w, conv_b, w_down):
    u = depthwise_conv(h @ w_up, conv_w) + conv_b
    gate, val = jnp.split(u, 2, axis=-1)
    return (jax.nn.silu(gate) * val) @ w_down


def setup_inputs(seed: int = 0) -> dict:
    key = jax.random.key(seed)
    ks = jax.random.split(key, 24)
    f32 = jnp.float32
    L = DEPTH

    def normal(k, shape, scale):
        return scale * jax.random.normal(k, shape, f32)

    def gain(k, shape):
        return 1.0 + 0.05 * jax.random.normal(k, shape, f32)

    a_log = jnp.log(jax.random.uniform(ks[12], (2, L, H_GDN), f32, 1.0, 16.0))
    dt = jnp.exp(jax.random.uniform(ks[13], (2, L, H_GDN), f32, float(np.log(1e-3)), float(np.log(1e-1))))
    dt_bias = dt + jnp.log(-jnp.expm1(-dt))
    return {
        "x": normal(ks[0], (BATCH, SEQ, D_MODEL), 1.0),
        "c": normal(ks[1], (BATCH, D_MODEL), 1.0),
        "ctx": normal(ks[2], (BATCH, CTX_LEN, D_MODEL), 1.0),
        "c_ctx": normal(ks[3], (D_MODEL,), 1.0),
        "w_mod": normal(ks[4], (L, D_MODEL, N_MOD * D_MODEL), 0.5 * D_MODEL ** -0.5),
        "b_mod": normal(ks[5], (L, N_MOD * D_MODEL), 0.02),
        "norm1_g": gain(ks[6], (L, D_MODEL)),
        "w_in": normal(ks[7], (L, D_MODEL, IN_WIDTH), D_MODEL ** -0.5),
        "q_norm_g": gain(ks[8], (L, HD_ATTN)),
        "k_norm_g": gain(ks[9], (L, HD_ATTN)),
        "attn_out_g": gain(ks[10], (L, ATTN_WIDTH)),
        "conv_qkv_w": normal(ks[11], (L, SHORT_CONV, GDN_QKV_WIDTH), SHORT_CONV ** -0.5),
        "a_log_f": a_log[0],
        "a_log_b": a_log[1],
        "dt_bias_f": dt_bias[0],
        "dt_bias_b": dt_bias[1],
        "gdn_norm_g": gain(ks[14], (L, DV_GDN)),
        "w_out": normal(ks[15], (L, MIX_WIDTH, D_MODEL), MIX_WIDTH ** -0.5),
        "norm2_g": gain(ks[16], (L, D_MODEL)),
        "w_up": normal(ks[17], (L, D_MODEL, 2 * D_FF), D_MODEL ** -0.5),
        "ffn_conv_w": normal(ks[18], (L, FFN_CONV, 2 * D_FF), FFN_CONV ** -0.5),
        "ffn_conv_b": normal(ks[19], (L, 2 * D_FF), 0.02),
        "w_down": normal(ks[20], (L, D_FF, D_MODEL), D_FF ** -0.5),
        "final_norm_g": gain(ks[21], (D_MODEL,)),
    }


def reference(x, c, ctx, c_ctx, w_mod, b_mod, norm1_g, w_in, q_norm_g, k_norm_g, attn_out_g,
              conv_qkv_w, a_log_f, a_log_b, dt_bias_f, dt_bias_b, gdn_norm_g, w_out, norm2_g,
              w_up, ffn_conv_w, ffn_conv_b, w_down, final_norm_g):
    rope = axial_rope_tables(x.shape[1])
    xc = ctx
    for l in range(DEPTH):
        last = l == DEPTH - 1
        sh1, sc1, g1, sh2, sc2, g2 = ada_terms(c, w_mod[l], b_mod[l])
        cmod = ada_terms(c_ctx, w_mod[l], b_mod[l])
        h = modulate(x, norm1_g[l], sh1, sc1)
        hc = modulate(xc, norm1_g[l], cmod[0], cmod[1])
        lat_out, ctx_out = hybrid_mixer(
            h, hc, rope, w_in[l], q_norm_g[l], k_norm_g[l], attn_out_g[l], conv_qkv_w[l],
            a_log_f[l], a_log_b[l], dt_bias_f[l], dt_bias_b[l], gdn_norm_g[l], w_out[l],
            not last)
        x = x + g1 * lat_out
        x = x + g2 * conv_ffn(modulate(x, norm2_g[l], sh2, sc2), w_up[l], ffn_conv_w[l],
                              ffn_conv_b[l], w_down[l])
        if not last:
            xc = xc + cmod[2] * ctx_out
            xc = xc + cmod[5] * conv_ffn(modulate(xc, norm2_g[l], cmod[3], cmod[4]), w_up[l],
                                         ffn_conv_w[l], ffn_conv_b[l], w_down[l])
    return rms_norm(x, final_norm_g)
```

```python
import functools

import jax
import jax.numpy as jnp
import numpy as np
from jax import lax
from jax.experimental import pallas as pl
from jax.experimental.pallas import tpu as pltpu

F32 = jnp.float32
BF16 = jnp.bfloat16

GRID_W = 64
H_ATTN = 8
KV_ATTN = 2
HD = 64
GQA = H_ATTN // KV_ATTN
ATTN_W = H_ATTN * HD
KV_W = KV_ATTN * HD
H_GDN = 4
DK = 128
GDN_QK = H_GDN * DK
GDN_QKV_W = 3 * GDN_QK
GDN_W = H_GDN * DK
N_MOD = 6
ROPE_THETA = 10000.0
EPS = 1e-6

LANES = 128
SUBLANES = 8
VMEM_LIMIT_BYTES = 56 * 1024 * 1024

CHUNK = 128

COL_Q = 0
COL_K = COL_Q + ATTN_W
COL_V = COL_K + KV_W
COL_G = COL_V + KV_W
COL_Z = COL_G + GDN_QKV_W
COL_AB = COL_Z + GDN_W
PACK_W = COL_AB + LANES


def _cparams(sem):
    return pltpu.CompilerParams(dimension_semantics=sem, vmem_limit_bytes=VMEM_LIMIT_BYTES)


def _silu(x):
    return x * jax.nn.sigmoid(x)


def _rms(x, gain):
    return x * lax.rsqrt(jnp.mean(x * x, axis=-1, keepdims=True) + EPS) * gain


def _mod_kernel(c_ref, w_ref, b_ref, o_ref):
    s = _silu(c_ref[...])
    o_ref[...] = jnp.dot(s, w_ref[...], preferred_element_type=F32,
                         precision=lax.Precision.HIGHEST) + b_ref[...]


def _mod_call(cc, w_mod, b_mod):
    rows, d = cc.shape
    n = w_mod.shape[1]
    bn = 1024
    return pl.pallas_call(
        _mod_kernel,
        grid=(n // bn,),
        in_specs=[pl.BlockSpec((rows, d), lambda j: (0, 0)),
                  pl.BlockSpec((d, bn), lambda j: (0, j)),
                  pl.BlockSpec((1, bn), lambda j: (0, j))],
        out_specs=pl.BlockSpec((rows, bn), lambda j: (0, j)),
        out_shape=jax.ShapeDtypeStruct((rows, n), F32),
        compiler_params=_cparams(("arbitrary",)),
        name="mod",
    )(cc, w_mod, b_mod.reshape(1, n))


def _low_half(shape):
    return (lax.broadcasted_iota(jnp.int32, shape, 1) & HD) == 0


def _head_norm(t, gain):
    lo = _low_half(t.shape)
    sq = t * t
    s_lo = jnp.sum(jnp.where(lo, sq, 0.0), axis=-1, keepdims=True)
    s_hi = jnp.sum(jnp.where(lo, 0.0, sq), axis=-1, keepdims=True)
    ms = jnp.where(lo, s_lo, s_hi) * (1.0 / HD)
    return t * lax.rsqrt(ms + EPS) * gain


def _rope(t, cos, sin_signed):
    first = (lax.broadcasted_iota(jnp.int32, t.shape, 1) & (HD // 2)) == 0
    partner = jnp.where(first, pltpu.roll(t, LANES - HD // 2, 1), pltpu.roll(t, HD // 2, 1))
    return t * cos + partner * sin_signed


def _dup_heads(t):
    lo = _low_half(t.shape)
    sw = pltpu.roll(t, HD, 1)
    return jnp.where(lo, t, sw), jnp.where(lo, sw, t)


def _inproj_kernel(*refs, latent):
    if latent:
        (x_ref, sh_ref, sc_ref, g_ref, w_ref, qg_ref, kg_ref, cos_ref, sin_ref,
         q_ref, k_ref, v_ref, gd_ref, z_ref, ab_ref) = refs
    else:
        (x_ref, sh_ref, sc_ref, g_ref, w_ref, kg_ref,
         k_ref, v_ref, gd_ref, ab_ref) = refs
    x = x_ref[...]
    h = _rms(x, g_ref[...]) * (1.0 + sc_ref[...]) + sh_ref[...]
    y = jnp.dot(h.astype(BF16), w_ref[...], preferred_element_type=F32)

    kt = _head_norm(y[:, COL_K:COL_K + KV_W], kg_ref[...])
    if latent:
        cos = cos_ref[...]
        sin = sin_ref[...]
        kt = _rope(kt, cos, sin)
        for i in range(ATTN_W // LANES):
            qt = y[:, COL_Q + i * LANES:COL_Q + (i + 1) * LANES]
            qt = _rope(_head_norm(qt, qg_ref[...]), cos, sin) * (HD ** -0.5)
            q_ref[:, i * LANES:(i + 1) * LANES] = qt.astype(BF16)
        z_ref[...] = y[:, COL_Z:COL_Z + GDN_W]
    k0, k1 = _dup_heads(kt)
    k_ref[0] = k0.astype(BF16)
    k_ref[1] = k1.astype(BF16)
    v0, v1 = _dup_heads(y[:, COL_V:COL_V + KV_W])
    v_ref[0] = v0.astype(BF16)
    v_ref[1] = v1.astype(BF16)
    gd_ref[...] = y[:, COL_G:COL_G + GDN_QKV_W]
    ab_ref[...] = y[:, COL_AB:COL_AB + LANES]


def _inproj_call(x, shift, scale, gain, w_pack, qg, kg, cos, sin, *, latent, tm):
    b, t, d = x.shape
    nt = t // tm
    per_batch = shift.shape[0] == b
    mod_map = (lambda i, j: (i, 0, 0)) if per_batch else (lambda i, j: (0, 0, 0))
    row = lambda i, j: (i, j, 0)
    const2 = lambda i, j: (0, 0)
    in_specs = [pl.BlockSpec((None, tm, d), row),
                pl.BlockSpec((None, 1, d), mod_map),
                pl.BlockSpec((None, 1, d), mod_map),
                pl.BlockSpec((1, d), const2),
                pl.BlockSpec((d, PACK_W), const2)]
    args = [x, shift, scale, gain, w_pack]
    if latent:
        in_specs += [pl.BlockSpec((1, LANES), const2), pl.BlockSpec((1, LANES), const2),
                     pl.BlockSpec((tm, LANES), lambda i, j: (j, 0)),
                     pl.BlockSpec((tm, LANES), lambda i, j: (j, 0))]
        args += [qg, kg, cos, sin]
    else:
        in_specs += [pl.BlockSpec((1, LANES), const2)]
        args += [kg]
    kv_spec = pl.BlockSpec((None, KV_ATTN, tm, LANES), lambda i, j: (i, 0, j, 0))
    kv_shape = jax.ShapeDtypeStruct((b, KV_ATTN, t, LANES), BF16)
    out_specs = [kv_spec, kv_spec, pl.BlockSpec((None, tm, GDN_QKV_W), row)]
    out_shape = [kv_shape, kv_shape, jax.ShapeDtypeStruct((b, t, GDN_QKV_W), F32)]
    if latent:
        out_specs = [pl.BlockSpec((None, tm, ATTN_W), row)] + out_specs + [pl.BlockSpec((None, tm, GDN_W), row)]
        out_shape = [jax.ShapeDtypeStruct((b, t, ATTN_W), BF16)] + out_shape + [jax.ShapeDtypeStruct((b, t, GDN_W), F32)]
    out_specs.append(pl.BlockSpec((None, tm, LANES), row))
    out_shape.append(jax.ShapeDtypeStruct((b, t, LANES), F32))
    return pl.pallas_call(
        functools.partial(_inproj_kernel, latent=latent),
        grid=(b, nt),
        in_specs=in_specs,
        out_specs=out_specs,
        out_shape=out_shape,
        compiler_params=_cparams(("parallel", "parallel")),
        name="inproj_lat" if latent else "inproj_ctx",
    )(*args)


def _attn_kernel(q_ref, kc_ref, vc_ref, kl_ref, vl_ref, o_ref, *, tq, tk):
    lo = _low_half((tq, LANES))
    zero = jnp.zeros((), BF16)
    parts = []
    for i in range(GQA * HD // LANES):
        qt = q_ref[:, i * LANES:(i + 1) * LANES]
        parts.append(jnp.where(lo, qt, zero))
        parts.append(jnp.where(lo, zero, qt))
    q4 = jnp.concatenate(parts, axis=0)
    m_rows = q4.shape[0]

    segs = [(kc_ref, vc_ref, 0, kc_ref.shape[0])]
    n_lat = kl_ref.shape[0]
    segs += [(kl_ref, vl_ref, c * tk, tk) for c in range(n_lat // tk)]

    m = jnp.full((m_rows, 1), -jnp.inf, F32)
    l = jnp.zeros((m_rows, 1), F32)
    acc = jnp.zeros((m_rows, LANES), F32)
    for k_ref, v_ref, start, size in segs:
        kb = k_ref[start:start + size, :]
        vb = v_ref[start:start + size, :]
        s = lax.dot_general(q4, kb, (((1,), (1,)), ((), ())), preferred_element_type=F32)
        m_new = jnp.maximum(m, jnp.max(s, axis=-1, keepdims=True))
        alpha = jnp.exp(m - m_new)
        p = jnp.exp(s - m_new)
        l = alpha * l + jnp.sum(p, axis=-1, keepdims=True)
        acc = alpha * acc + jnp.dot(p.astype(BF16), vb, preferred_element_type=F32)
        m = m_new
    o = acc / l
    for i in range(GQA * HD // LANES):
        a = o[(2 * i) * tq:(2 * i + 1) * tq, :]
        b = o[(2 * i + 1) * tq:(2 * i + 2) * tq, :]
        o_ref[:, i * LANES:(i + 1) * LANES] = jnp.where(lo, a, b)


def _attn_call(q, kc, vc, kl, vl, *, tq, tk):
    b, t, _ = q.shape
    tc = kc.shape[2]
    gw = GQA * HD
    kv_c = pl.BlockSpec((None, None, tc, LANES), lambda i, j, n: (i, j, 0, 0))
    kv_l = pl.BlockSpec((None, None, t, LANES), lambda i, j, n: (i, j, 0, 0))
    qo = pl.BlockSpec((None, tq, gw), lambda i, j, n: (i, n, j))
    return pl.pallas_call(
        functools.partial(_attn_kernel, tq=tq, tk=tk),
        grid=(b, KV_ATTN, t // tq),
        in_specs=[qo, kv_c, kv_c, kv_l, kv_l],
        out_specs=qo,
        out_shape=jax.ShapeDtypeStruct((b, t, ATTN_W), F32),
        compiler_params=_cparams(("parallel", "parallel", "arbitrary")),
        name="attn",
    )(q, kc, vc, kl, vl)


def _gdn_prep_kernel(q_ref, k_ref, v_ref, qp_ref, kp_ref, vp_ref, qn_ref, kn_ref, vn_ref,
                     cq_ref, ck_ref, cv_ref, ab_ref, par_ref,
                     wq_ref, ik_ref, u_ref, dl_ref, *, tb):
    c = CHUNK
    t_idx = pl.program_id(2)
    nt = pl.num_programs(2)
    h_idx = pl.program_id(1)
    row = lax.broadcasted_iota(jnp.int32, (tb, LANES), 0)

    def conv_act(x_ref, xp_ref, xn_ref, w_ref):
        x = x_ref[...]
        prev_row = jnp.where(t_idx > 0, xp_ref[SUBLANES - 1:SUBLANES, :], 0.0)
        next_row = jnp.where(t_idx < nt - 1, xn_ref[0:1, :], 0.0)
        xm = jnp.where(row == 0, prev_row, pltpu.roll(x, 1, 0))
        xp = jnp.where(row == tb - 1, next_row, pltpu.roll(x, tb - 1, 0))
        w = w_ref[...]
        y = xm * w[0:1, :] + x * w[1:2, :] + xp * w[2:3, :]
        return _silu(y)

    def l2n(x):
        return x * lax.rsqrt(jnp.sum(x * x, axis=-1, keepdims=True) + EPS)

    q = l2n(conv_act(q_ref, qp_ref, qn_ref, cq_ref)) * (DK ** -0.5)
    k = l2n(conv_act(k_ref, kp_ref, kn_ref, ck_ref))
    v = conv_act(v_ref, vp_ref, vn_ref, cv_ref)

    ab = ab_ref[...]
    par = par_ref[...]
    g_all = -jnp.exp(par[0:1, :]) * jax.nn.softplus(ab + par[1:2, :])
    beta_all = jax.nn.sigmoid(ab)

    ri = lax.broadcasted_iota(jnp.int32, (c, c), 0)
    ci = lax.broadcasted_iota(jnp.int32, (c, c), 1)
    eye = (ri == ci).astype(F32)
    xr = ri ^ ci
    lane =lax.broadcasted_iota(jnp.int32, (c, LANES), 1)

    for n in range(tb // c):
        sl = slice(n * c, (n + 1) * c)
        qc, kc, vc = q[sl], k[sl], v[sl]
        g_c = g_all[sl]
        beta_c = beta_all[sl]
        tri_f = (ri >= ci).astype(F32)
        tri_b = (ri <= ci).astype(F32)
        cs_f = jnp.dot(tri_f, g_c, preferred_element_type=F32, precision=lax.Precision.HIGHEST)
        cs_b = jnp.dot(tri_b, g_c, preferred_element_type=F32, precision=lax.Precision.HIGHEST)
        gc_all = jnp.where((lane & H_GDN) == 0, cs_f, cs_b)
        gc_t = gc_all.T
        kq = jnp.concatenate([kc, qc], axis=0).astype(BF16)
        kkqk = lax.dot_general(kq, kc.astype(BF16), (((1,), (1,)), ((), ())),
                               preferred_element_type=F32)
        kk, qk = kkqk[:c], kkqk[c:]
        k_t = kc.T
        for d in range(2):
            sel_g = lane == d * H_GDN + h_idx
            sel_b = lane == 2 * H_GDN + d * H_GDN + h_idx
            gcol = jnp.sum(jnp.where(sel_g, gc_all, 0.0), axis=-1, keepdims=True)
            bcol = jnp.sum(jnp.where(sel_b, beta_c, 0.0), axis=-1, keepdims=True)
            rsel = lax.broadcasted_iota(jnp.int32, (LANES, c), 0) == d * H_GDN + h_idx
            grow = jnp.sum(jnp.where(rsel, gc_t, 0.0), axis=0, keepdims=True)
            incl = (ri >= ci) if d == 0 else (ri <= ci)
            strict = (ri > ci) if d == 0 else (ri < ci)
            dec = jnp.exp(jnp.where(incl, gcol - grow, -jnp.inf))
            a_mat = jnp.where(strict, kk * bcol * dec, 0.0)
            inv = eye - jnp.where((xr >> 1) == 0, a_mat, 0.0)
            for lvl in range(1, int(np.log2(c))):
                off = jnp.where(((xr >> (lvl + 1)) == 0) & ((xr >> lvl) != 0), a_mat, 0.0)
                invb = inv.astype(BF16)
                mt = jnp.dot(off.astype(BF16), invb, preferred_element_type=F32)
                inv = inv - jnp.dot(invb, mt.astype(BF16), preferred_element_type=F32)
            egc = jnp.exp(gcol)
            rhs = jnp.concatenate([vc * bcol, kc * (bcol * egc)], axis=1).astype(BF16)
            uw = jnp.dot(inv.astype(BF16), rhs, preferred_element_type=F32)
            glast = gcol[c - 1:c, :] if d == 0 else gcol[0:1, :]
            u_ref[d, sl, :] = uw[:, :DK]
            wq_ref[d, n, 0:c, :] = uw[:, DK:].astype(BF16)
            wq_ref[d, n, c:2 * c, :] = (qc * egc).astype(BF16)
            ik_ref[d, n, 0:c, :] = (qk * dec).astype(BF16)
            ik_ref[d, n, c:c + DK, :] = (k_t * jnp.exp(glast - grow)).astype(BF16)
            dl_ref[d, n] = jnp.broadcast_to(jnp.exp(glast), (SUBLANES, LANES))


def _gdn_prep_call(gd, conv_w, ab, par, *, tb):
    b, t, _ = gd.shape
    c = CHUNK
    nchunk = t // c
    nb8 = t // SUBLANES
    tb8 = tb // SUBLANES

    def main(off):
        return pl.BlockSpec((None, tb, DK), lambda i, h, j: (i, j, off + h))

    def prev(off):
        return pl.BlockSpec((None, SUBLANES, DK),
                            lambda i, h, j: (i, jnp.maximum(j * tb8 - 1, 0), off + h))

    def nxt(off):
        return pl.BlockSpec((None, SUBLANES, DK),
                            lambda i, h, j: (i, jnp.minimum((j + 1) * tb8, nb8 - 1), off + h))

    def cw(off):
        return pl.BlockSpec((3, DK), lambda i, h, j: (0, off + h))

    in_specs = [main(0), main(H_GDN), main(2 * H_GDN),
                prev(0), prev(H_GDN), prev(2 * H_GDN),
                nxt(0), nxt(H_GDN), nxt(2 * H_GDN),
                cw(0), cw(H_GDN), cw(2 * H_GDN),
                pl.BlockSpec((None, tb, LANES), lambda i, h, j: (i, j, 0)),
                pl.BlockSpec((SUBLANES, LANES), lambda i, h, j: (0, 0))]
    ncb = tb // c
    out_specs = [
        pl.BlockSpec((None, 2, None, ncb, 2 * c, DK), lambda i, h, j: (i, 0, h, j, 0, 0)),
        pl.BlockSpec((None, 2, None, ncb, c + DK, c), lambda i, h, j: (i, 0, h, j, 0, 0)),
        pl.BlockSpec((None, 2, None, tb, DK), lambda i, h, j: (i, 0, h, j, 0)),
        pl.BlockSpec((None, 2, None, ncb, SUBLANES, LANES), lambda i, h, j: (i, 0, h, j, 0, 0)),
    ]
    out_shape = [
        jax.ShapeDtypeStruct((b, 2, H_GDN, nchunk, 2 * c, DK), BF16),
        jax.ShapeDtypeStruct((b, 2, H_GDN, nchunk, c + DK, c), BF16),
        jax.ShapeDtypeStruct((b, 2, H_GDN, t, DK), F32),
        jax.ShapeDtypeStruct((b, 2, H_GDN, nchunk, SUBLANES, LANES), F32),
    ]
    return pl.pallas_call(
        functools.partial(_gdn_prep_kernel, tb=tb),
        grid=(b, H_GDN, t // tb),
        in_specs=in_specs,
        out_specs=out_specs,
        out_shape=out_shape,
        compiler_params=_cparams(("parallel", "parallel", "arbitrary")),
        name="gdn_prep",
    )(gd, gd, gd, gd, gd, gd, gd, gd, gd, conv_w, conv_w, conv_w, ab, par)


def _gdn_scan_kernel(wqf_ref, ikf_ref, uf_ref, dlf_ref, wqb_ref, ikb_ref, ub_ref, dlb_ref, s0_ref,
                     *out_refs, nb, with_out):
    c = CHUNK
    if with_out:
        of_ref, ob_ref, sfin_ref, s_ref = out_refs
    else:
        sfin_ref, s_ref = out_refs
    step = pl.program_id(1)

    @pl.when(step == 0)
    def _():
        s_ref[...] = s0_ref[...]

    for n in range(nb):
        for d in range(2):
            wq_ref, ik_ref, u_ref, dl_ref = ((wqf_ref, ikf_ref, uf_ref, dlf_ref) if d == 0
                                             else (wqb_ref, ikb_ref, ub_ref, dlb_ref))
            cn = n if d == 0 else nb - 1 - n
            for h in range(H_GDN):
                s = s_ref[d, h]
                r1 = jnp.dot(wq_ref[h, cn], s.astype(BF16), preferred_element_type=F32)
                v_new = u_ref[h, cn * c:(cn + 1) * c, :] - r1[:c]
                r2 = jnp.dot(ik_ref[h, cn], v_new.astype(BF16), preferred_element_type=F32)
                if with_out:
                    o_ref = of_ref if d == 0 else ob_ref
                    o_ref[cn * c:(cn + 1) * c, h * DK:(h + 1) * DK] = r1[c:] + r2[:c]
                s_ref[d, h] = s * dl_ref[h, cn, 0:1, 0:1] + r2[c:]

    @pl.when(step == pl.num_programs(1) - 1)
    def _():
        sfin_ref[...] = s_ref[...]


def _gdn_scan_call(wq, ik, u, dl, s0, *, nb, with_out):
    b = wq.shape[0]
    c = CHUNK
    nchunk = wq.shape[3]
    t = nchunk * c
    ns = nchunk // nb

    def spec(shape_tail, d, tok_scale):
        nd = len(shape_tail)
        if d == 0:
            return pl.BlockSpec((None, None, H_GDN, tok_scale) + shape_tail,
                                lambda i, j: (i, 0, 0, j) + (0,) * nd)
        return pl.BlockSpec((None, None, H_GDN, tok_scale) + shape_tail,
                            lambda i, j: (i, 1, 0, ns - 1 - j) + (0,) * nd)

    in_specs = []
    for d in range(2):
        in_specs += [spec((2 * c, DK), d, nb), spec((c + DK, c), d, nb),
                     spec((DK,), d, nb * c), spec((SUBLANES, LANES), d, nb)]
    in_specs.append(pl.BlockSpec((None, 2, H_GDN, DK, DK), lambda i, j: (i, 0, 0, 0, 0)))
    out_specs = []
    out_shape = []
    if with_out:
        out_specs += [pl.BlockSpec((None, nb * c, GDN_W), lambda i, j: (i, j, 0)),
                      pl.BlockSpec((None, nb * c, GDN_W), lambda i, j: (i, ns - 1 - j, 0))]
        out_shape += [jax.ShapeDtypeStruct((b, t, GDN_W), F32)] * 2
    out_specs.append(pl.BlockSpec((None, 2, H_GDN, DK, DK), lambda i, j: (i, 0, 0, 0, 0)))
    out_shape.append(jax.ShapeDtypeStruct((b, 2, H_GDN, DK, DK), F32))
    return pl.pallas_call(
        functools.partial(_gdn_scan_kernel, nb=nb, with_out=with_out),
        grid=(b, ns),
        in_specs=in_specs,
        out_specs=out_specs,
        out_shape=out_shape,
        scratch_shapes=[pltpu.VMEM((2, H_GDN, DK, DK), F32)],
        compiler_params=_cparams(("parallel", "arbitrary")),
        name="gdn_scan" if with_out else "gdn_scan_ctx",
    )(wq, ik, u, dl, wq, ik, u, dl, s0)


def _outproj_kernel(x_ref, oa_ref, of_ref, ob_ref, z_ref, g1_ref, ag_ref, gg_ref, w_ref, o_ref):
    oa = _rms(oa_ref[...], ag_ref[...])
    og = of_ref[...] + ob_ref[...]
    z = z_ref[...]
    gg = gg_ref[...]
    parts = [oa.astype(BF16)]
    for h in range(H_GDN):
        sl = slice(h * DK, (h + 1) * DK)
        parts.append((_rms(og[:, sl], gg) * _silu(z[:, sl])).astype(BF16))
    cat = jnp.concatenate(parts, axis=1)
    lat = jnp.dot(cat, w_ref[...], preferred_element_type=F32)
    o_ref[...] = x_ref[...] + g1_ref[...] * lat


def _outproj_call(x, oa, of, ob, z, g1, ag, gg, w_out, *, tm):
    b, t, d = x.shape
    row = lambda i, j: (i, j, 0)
    const2 = lambda i, j: (0, 0)
    return pl.pallas_call(
        _outproj_kernel,
        grid=(b, t // tm),
        in_specs=[pl.BlockSpec((None, tm, d), row),
                  pl.BlockSpec((None, tm, ATTN_W), row),
                  pl.BlockSpec((None, tm, GDN_W), row),
                  pl.BlockSpec((None, tm, GDN_W), row),
                  pl.BlockSpec((None, tm, GDN_W), row),
                  pl.BlockSpec((None, 1, d), lambda i, j: (i, 0, 0)),
                  pl.BlockSpec((1, ATTN_W), const2),
                  pl.BlockSpec((1, DK), const2),
                  pl.BlockSpec((ATTN_W + GDN_W, d), const2)],
        out_specs=pl.BlockSpec((None, tm, d), row),
        out_shape=jax.ShapeDtypeStruct((b, t, d), F32),
        compiler_params=_cparams(("parallel", "parallel")),
        name="outproj",
    )(x, oa, of, ob, z, g1, ag, gg, w_out)


def _ffn_kernel(x_ref, xp_ref, xn_ref, sh_ref, sc_ref, g2_ref, ng_ref, wu_ref, cw_ref, cb_ref,
                wd_ref, fg_ref, o_ref, *, tm, nc):
    t_idx = pl.program_id(1)
    nt = pl.num_programs(1)
    d_ff = wd_ref.shape[0]
    x = x_ref[...]
    ng, sc, sh = ng_ref[...], sc_ref[...], sh_ref[...]
    h = (_rms(x, ng) * (1.0 + sc) + sh).astype(BF16)
    halo = jnp.concatenate([xp_ref[...], xn_ref[...]], axis=0)
    hh = (_rms(halo, ng) * (1.0 + sc) + sh).astype(BF16)
    row = lax.broadcasted_iota(jnp.int32, (tm, nc), 0)
    has_prev = t_idx > 0
    has_next = t_idx < nt - 1
    acc = jnp.zeros(x.shape, F32)
    for c in range(d_ff // nc):
        us = []
        for base in (c * nc, d_ff + c * nc):
            wsl = wu_ref[:, base:base + nc]
            up = jnp.dot(h, wsl, preferred_element_type=F32)
            uph = jnp.dot(hh, wsl, preferred_element_type=F32)
            prev_row = jnp.where(has_prev, uph[SUBLANES - 1:SUBLANES, :], 0.0)
            next_row = jnp.where(has_next, uph[SUBLANES:SUBLANES + 1, :], 0.0)
            um = jnp.where(row == 0, prev_row, pltpu.roll(up, 1, 0))
            upn = jnp.where(row == tm - 1, next_row, pltpu.roll(up, tm - 1, 0))
            w = cw_ref[:, base:base + nc]
            us.append(um * w[0:1, :] + up * w[1:2, :] + upn * w[2:3, :] + cb_ref[:, base:base + nc])
        act = (_silu(us[0]) * us[1]).astype(BF16)
        acc = acc + jnp.dot(act, wd_ref[c * nc:(c + 1) * nc, :], preferred_element_type=F32)
    x2 = x + g2_ref[...] * acc
    o_ref[...] = _rms(x2, fg_ref[...])


def _ffn_call(x, sh2, sc2, g2, ng, w_up, conv_w, conv_b, w_down, fg, *, tm, nc):
    b, t, d = x.shape
    d_ff = w_down.shape[0]
    nb8 = t // SUBLANES
    tm8 = tm // SUBLANES
    row = lambda i, j: (i, j, 0)
    mod = lambda i, j: (i, 0, 0)
    const2 = lambda i, j: (0, 0)
    return pl.pallas_call(
        functools.partial(_ffn_kernel, tm=tm, nc=nc),
        grid=(b, t // tm),
        in_specs=[pl.BlockSpec((None, tm, d), row),
                  pl.BlockSpec((None, SUBLANES, d), lambda i, j: (i, jnp.maximum(j * tm8 - 1, 0), 0)),
                  pl.BlockSpec((None, SUBLANES, d), lambda i, j: (i, jnp.minimum((j + 1) * tm8, nb8 - 1), 0)),
                  pl.BlockSpec((None, 1, d), mod),
                  pl.BlockSpec((None, 1, d), mod),
                  pl.BlockSpec((None, 1, d), mod),
                  pl.BlockSpec((1, d), const2),
                  pl.BlockSpec((d, 2 * d_ff), const2),
                  pl.BlockSpec((3, 2 * d_ff), const2),
                  pl.BlockSpec((1, 2 * d_ff), const2),
                  pl.BlockSpec((d_ff, d), const2),
                  pl.BlockSpec((1, d), const2)],
        out_specs=pl.BlockSpec((None, tm, d), row),
        out_shape=jax.ShapeDtypeStruct((b, t, d), F32),
        compiler_params=_cparams(("parallel", "parallel")),
        name="ffn",
    )(x, x, x, sh2, sc2, g2, ng, w_up, conv_w, conv_b, w_down, fg)


def _rope_tables(t):
    rows = t // GRID_W
    r = jnp.repeat(jnp.arange(rows, dtype=F32), GRID_W)
    c = jnp.tile(jnp.arange(GRID_W, dtype=F32), rows)
    pairs = HD // 4
    inv_freq = ROPE_THETA ** (-jnp.arange(pairs, dtype=F32) / pairs)
    ang = jnp.concatenate([r[:, None] * inv_freq, c[:, None] * inv_freq], axis=-1)
    cos, sin = jnp.cos(ang), jnp.sin(ang)
    cos128 = jnp.tile(cos, (1, LANES // (HD // 2)))
    sin128 = jnp.tile(jnp.concatenate([-sin, sin], axis=-1), (1, LANES // HD))
    return cos128, sin128


def _pick(n, candidates):
    for c in candidates:
        if n % c == 0:
            return c
    raise ValueError(f"no tile size in {candidates} divides {n}")


def kernel(x, c, ctx, c_ctx, w_mod, b_mod, norm1_g, w_in, q_norm_g, k_norm_g, attn_out_g,
           conv_qkv_w, a_log_f, a_log_b, dt_bias_f, dt_bias_b, gdn_norm_g, w_out, norm2_g,
           w_up, ffn_conv_w, ffn_conv_b, w_down, final_norm_g):
    depth = w_mod.shape[0]
    assert depth == 1, "only the single-layer configuration is implemented"
    b, t, d = x.shape
    tc = ctx.shape[1]
    assert t % CHUNK == 0 and tc % CHUNK == 0 and t % GRID_W == 0

    rows = -(-(b + 1) // SUBLANES) * SUBLANES
    cc = jnp.zeros((rows, d), F32).at[:b].set(c).at[b].set(c_ctx)
    mod = _mod_call(cc, w_mod[0], b_mod[0]).reshape(rows, N_MOD, d)
    lat = [mod[:b, i][:, None, :] for i in range(N_MOD)]
    cm = [mod[b:b + 1, i][:, None, :] for i in range(2)]

    w_pack = jnp.pad(w_in[0], ((0, 0), (0, PACK_W - w_in.shape[-1]))).astype(BF16)
    qg = jnp.tile(q_norm_g[0], LANES // HD)[None, :]
    kg = jnp.tile(k_norm_g[0], LANES // HD)[None, :]
    cos, sin = _rope_tables(t)
    g1n = norm1_g[0][None, :]

    tm = _pick(t, (512, 256, 128))
    q, kl, vl, gdl, z, abl = _inproj_call(x, lat[0], lat[1], g1n, w_pack, qg, kg, cos, sin,
                                          latent=True, tm=tm)
    kc, vc, gdc, abc = _inproj_call(ctx, cm[0], cm[1], g1n, w_pack, None, kg, None, None,
                                    latent=False, tm=_pick(tc, (256, 128)))

    o_attn = _attn_call(q, kc, vc, kl, vl, tq=_pick(t, (256, 128)), tk=_pick(t, (512, 256, 128)))

    par = jnp.zeros((SUBLANES, LANES), F32)
    par = par.at[0, :2 * H_GDN].set(jnp.concatenate([a_log_f[0], a_log_b[0]]))
    par = par.at[1, :2 * H_GDN].set(jnp.concatenate([dt_bias_f[0], dt_bias_b[0]]))
    cw = conv_qkv_w[0]
    prep_c = _gdn_prep_call(gdc, cw, abc, par, tb=_pick(tc, (256, 128)))
    s0 = jnp.zeros((b, 2, H_GDN, DK, DK), F32)
    (s_ctx,) = _gdn_scan_call(*prep_c, s0, nb=tc // CHUNK, with_out=False)
    prep_l = _gdn_prep_call(gdl, cw, abl, par, tb=_pick(t, (256, 128)))
    o_f, o_b, _ = _gdn_scan_call(*prep_l, s_ctx, nb=_pick(t // CHUNK, (4, 2, 1)), with_out=True)

    x1 = _outproj_call(x, o_attn, o_f, o_b, z, lat[2], attn_out_g[0][None, :],
                       gdn_norm_g[0][None, :], w_out[0].astype(BF16), tm=tm)
    return _ffn_call(x1, lat[3], lat[4], lat[5], norm2_g[0][None, :], w_up[0].astype(BF16),
                     ffn_conv_w[0], ffn_conv_b[0][None, :], w_down[0].astype(BF16),
                     final_norm_g[None, :], tm=tm, nc=256)
```

```python
import functools

import jax
import jax.numpy as jnp
import numpy as np
from jax import lax
from jax.experimental import pallas as pl
from jax.experimental.pallas import tpu as pltpu

F32 = jnp.float32
BF16 = jnp.bfloat16

GRID_W = 64
H_ATTN = 8
KV_ATTN = 2
HD = 64
GQA = H_ATTN // KV_ATTN
ATTN_W = H_ATTN * HD
KV_W = KV_ATTN * HD
H_GDN = 4
DK = 128
GDN_QK = H_GDN * DK
GDN_QKV_W = 3 * GDN_QK
GDN_W = H_GDN * DK
N_MOD = 6
ROPE_THETA = 10000.0
EPS = 1e-6

LANES = 128
SUBLANES = 8
VMEM_LIMIT_BYTES = 56 * 1024 * 1024

CHUNK = 128

COL_Q = 0
COL_K = COL_Q + ATTN_W
COL_V = COL_K + KV_W
COL_G = COL_V + KV_W
COL_Z = COL_G + GDN_QKV_W
COL_AB = COL_Z + GDN_W
PACK_W = COL_AB + LANES


def _cparams(sem):
    return pltpu.CompilerParams(dimension_semantics=sem, vmem_limit_bytes=VMEM_LIMIT_BYTES)


def _silu(x):
    return x * jax.nn.sigmoid(x)


def _rms(x, gain):
    return x * lax.rsqrt(jnp.mean(x * x, axis=-1, keepdims=True) + EPS) * gain


def _mod_kernel(c_ref, w_ref, b_ref, o_ref):
    s = _silu(c_ref[...])
    o_ref[...] = jnp.dot(s, w_ref[...], preferred_element_type=F32,
                         precision=lax.Precision.HIGHEST) + b_ref[...]


def _mod_call(cc, w_mod, b_mod):
    rows, d = cc.shape
    n = w_mod.shape[1]
    bn = 1024
    return pl.pallas_call(
        _mod_kernel,
        grid=(n // bn,),
        in_specs=[pl.BlockSpec((rows, d), lambda j: (0, 0)),
                  pl.BlockSpec((d, bn), lambda j: (0, j)),
                  pl.BlockSpec((1, bn), lambda j: (0, j))],
        out_specs=pl.BlockSpec((rows, bn), lambda j: (0, j)),
        out_shape=jax.ShapeDtypeStruct((rows, n), F32),
        compiler_params=_cparams(("arbitrary",)),
        name="mod",
    )(cc, w_mod, b_mod.reshape(1, n))


def _low_half(shape):
    return (lax.broadcasted_iota(jnp.int32, shape, 1) & HD) == 0


def _head_norm(t, gain):
    lo = _low_half(t.shape)
    sq = t * t
    s_lo = jnp.sum(jnp.where(lo, sq, 0.0), axis=-1, keepdims=True)
    s_hi = jnp.sum(jnp.where(lo, 0.0, sq), axis=-1, keepdims=True)
    ms = jnp.where(lo, s_lo, s_hi) * (1.0 / HD)
    return t * lax.rsqrt(ms + EPS) * gain


def _rope(t, cos, sin_signed):
    first = (lax.broadcasted_iota(jnp.int32, t.shape, 1) & (HD // 2)) == 0
    partner = jnp.where(first, pltpu.roll(t, LANES - HD // 2, 1), pltpu.roll(t, HD // 2, 1))
    return t * cos + partner * sin_signed


def _dup_heads(t):
    lo = _low_half(t.shape)
    sw = pltpu.roll(t, HD, 1)
    return jnp.where(lo, t, sw), jnp.where(lo, sw, t)


def _inproj_kernel(*refs, latent):
    if latent:
        (x_ref, sh_ref, sc_ref, g_ref, w_ref, qg_ref, kg_ref, cos_ref, sin_ref,
         q_ref, k_ref, v_ref, gd_ref, z_ref, ab_ref) = refs
    else:
        (x_ref, sh_ref, sc_ref, g_ref, w_ref, kg_ref,
         k_ref, v_ref, gd_ref, ab_ref) = refs
    x = x_ref[...]
    h = _rms(x, g_ref[...]) * (1.0 + sc_ref[...]) + sh_ref[...]
    y = jnp.dot(h.astype(BF16), w_ref[...], preferred_element_type=F32)

    kt = _head_norm(y[:, COL_K:COL_K + KV_W], kg_ref[...])
    if latent:
        cos = cos_ref[...]
        sin = sin_ref[...]
        kt = _rope(kt, cos, sin)
        for i in range(ATTN_W // LANES):
            qt = y[:, COL_Q + i * LANES:COL_Q + (i + 1) * LANES]
            qt = _rope(_head_norm(qt, qg_ref[...]), cos, sin) * (HD ** -0.5)
            q_ref[:, i * LANES:(i + 1) * LANES] = qt.astype(BF16)
        z_ref[...] = y[:, COL_Z:COL_Z + GDN_W]
    k0, k1 = _dup_heads(kt)
    k_ref[0] = k0.astype(BF16)
    k_ref[1] = k1.astype(BF16)
    v0, v1 = _dup_heads(y[:, COL_V:COL_V + KV_W])
    v_ref[0] = v0.astype(BF16)
    v_ref[1] = v1.astype(BF16)
    gd_ref[...] = y[:, COL_G:COL_G + GDN_QKV_W]
    ab_ref[...] = y[:, COL_AB:COL_AB + LANES]


def _inproj_call(x, shift, scale, gain, w_pack, qg, kg, cos, sin, *, latent, tm):
    b, t, d = x.shape
    nt = t // tm
    per_batch = shift.shape[0] == b
    mod_map = (lambda i, j: (i, 0, 0)) if per_batch else (lambda i, j: (0, 0, 0))
    row = lambda i, j: (i, j, 0)
    const2 = lambda i, j: (0, 0)
    in_specs = [pl.BlockSpec((None, tm, d), row),
                pl.BlockSpec((None, 1, d), mod_map),
                pl.BlockSpec((None, 1, d), mod_map),
                pl.BlockSpec((1, d), const2),
                pl.BlockSpec((d, PACK_W), const2)]
    args = [x, shift, scale, gain, w_pack]
    if latent:
        in_specs += [pl.BlockSpec((1, LANES), const2), pl.BlockSpec((1, LANES), const2),
                     pl.BlockSpec((tm, LANES), lambda i, j: (j, 0)),
                     pl.BlockSpec((tm, LANES), lambda i, j: (j, 0))]
        args += [qg, kg, cos, sin]
    else:
        in_specs += [pl.BlockSpec((1, LANES), const2)]
        args += [kg]
    kv_spec = pl.BlockSpec((None, KV_ATTN, tm, LANES), lambda i, j: (i, 0, j, 0))
    kv_shape = jax.ShapeDtypeStruct((b, KV_ATTN, t, LANES), BF16)
    out_specs = [kv_spec, kv_spec, pl.BlockSpec((None, tm, GDN_QKV_W), row)]
    out_shape = [kv_shape, kv_shape, jax.ShapeDtypeStruct((b, t, GDN_QKV_W), F32)]
    if latent:
        out_specs = [pl.BlockSpec((None, tm, ATTN_W), row)] + out_specs + [pl.BlockSpec((None, tm, GDN_W), row)]
        out_shape = [jax.ShapeDtypeStruct((b, t, ATTN_W), BF16)] + out_shape + [jax.ShapeDtypeStruct((b, t, GDN_W), F32)]
    out_specs.append(pl.BlockSpec((None, tm, LANES), row))
    out_shape.append(jax.ShapeDtypeStruct((b, t, LANES), F32))
    return pl.pallas_call(
        functools.partial(_inproj_kernel, latent=latent),
        grid=(b, nt),
        in_specs=in_specs,
        out_specs=out_specs,
        out_shape=out_shape,
        compiler_params=_cparams(("parallel", "parallel")),
        name="inproj_lat" if latent else "inproj_ctx",
    )(*args)


def _attn_kernel(q_ref, kc_ref, vc_ref, kl_ref, vl_ref, o_ref, *, tq, tk):
    lo = _low_half((tq, LANES))
    zero = jnp.zeros((), BF16)
    parts = []
    for i in range(GQA * HD // LANES):
        qt = q_ref[:, i * LANES:(i + 1) * LANES]
        parts.append(jnp.where(lo, qt, zero))
        parts.append(jnp.where(lo, zero, qt))
    q4 = jnp.concatenate(parts, axis=0)
    m_rows = q4.shape[0]

    segs = [(kc_ref, vc_ref, 0, kc_ref.shape[0])]
    n_lat = kl_ref.shape[0]
    segs += [(kl_ref, vl_ref, c * tk, tk) for c in range(n_lat // tk)]

    m = jnp.full((m_rows, 1), -jnp.inf, F32)
    l = jnp.zeros((m_rows, 1), F32)
    acc = jnp.zeros((m_rows, LANES), F32)
    for k_ref, v_ref, start, size in segs:
        kb = k_ref[start:start + size, :]
        vb = v_ref[start:start + size, :]
        s = lax.dot_general(q4, kb, (((1,), (1,)), ((), ())), preferred_element_type=F32)
        m_new = jnp.maximum(m, jnp.max(s, axis=-1, keepdims=True))
        alpha = jnp.exp(m - m_new)
        p = jnp.exp(s - m_new)
        l = alpha * l + jnp.sum(p, axis=-1, keepdims=True)
        acc = alpha * acc + jnp.dot(p.astype(BF16), vb, preferred_element_type=F32)
        m = m_new
    o = acc / l
    for i in range(GQA * HD // LANES):
        a = o[(2 * i) * tq:(2 * i + 1) * tq, :]
        b = o[(2 * i + 1) * tq:(2 * i + 2) * tq, :]
        o_ref[:, i * LANES:(i + 1) * LANES] = jnp.where(lo, a, b)


def _attn_call(q, kc, vc, kl, vl, *, tq, tk):
    b, t, _ = q.shape
    tc = kc.shape[2]
    gw = GQA * HD
    kv_c = pl.BlockSpec((None, None, tc, LANES), lambda i, j, n: (i, j, 0, 0))
    kv_l = pl.BlockSpec((None, None, t, LANES), lambda i, j, n: (i, j, 0, 0))
    qo = pl.BlockSpec((None, tq, gw), lambda i, j, n: (i, n, j))
    return pl.pallas_call(
        functools.partial(_attn_kernel, tq=tq, tk=tk),
        grid=(b, KV_ATTN, t // tq),
        in_specs=[qo, kv_c, kv_c, kv_l, kv_l],
        out_specs=qo,
        out_shape=jax.ShapeDtypeStruct((b, t, ATTN_W), F32),
        compiler_params=_cparams(("parallel", "parallel", "arbitrary")),
        name="attn",
    )(q, kc, vc, kl, vl)


def _gdn_prep_kernel(x_ref, xp_ref, xn_ref, cw_ref, ab_ref, par_ref,
                     wq_ref, ik_ref, u_ref, dl_ref, *, tb):
    c = CHUNK
    t_idx = pl.program_id(1)
    nt = pl.num_programs(1)

    x = x_ref[...]
    row = lax.broadcasted_iota(jnp.int32, x.shape, 0)
    prev_row = jnp.where(t_idx > 0, xp_ref[SUBLANES - 1:SUBLANES, :], 0.0)
    next_row = jnp.where(t_idx < nt - 1, xn_ref[0:1, :], 0.0)
    xm = jnp.where(row == 0, prev_row, pltpu.roll(x, 1, 0))
    xp = jnp.where(row == tb - 1, next_row, pltpu.roll(x, tb - 1, 0))
    w = cw_ref[...]
    y = _silu(xm * w[0:1, :] + x * w[1:2, :] + xp * w[2:3, :])

    def l2n(a):
        return a * lax.rsqrt(jnp.sum(a * a, axis=-1, keepdims=True) + EPS)

    qs = [l2n(y[:, h * DK:(h + 1) * DK]) * (DK ** -0.5) for h in range(H_GDN)]
    ks = [l2n(y[:, GDN_QK + h * DK:GDN_QK + (h + 1) * DK]) for h in range(H_GDN)]
    vs = [y[:, 2 * GDN_QK + h * DK:2 * GDN_QK + (h + 1) * DK] for h in range(H_GDN)]

    ab = ab_ref[...]
    par = par_ref[...]
    g_all = -jnp.exp(par[0:1, :]) * jax.nn.softplus(ab + par[1:2, :])
    beta_all = jax.nn.sigmoid(ab)

    ri = lax.broadcasted_iota(jnp.int32, (c, c), 0)
    ci = lax.broadcasted_iota(jnp.int32, (c, c), 1)
    eye = (ri == ci).astype(F32)
    xr = ri ^ ci
    lane = lax.broadcasted_iota(jnp.int32, (c, LANES), 1)
    tri_f = (ri >= ci).astype(F32)
    tri_b = (ri <= ci).astype(F32)

    chains = []
    for n in range(tb // c):
        sl = slice(n * c, (n + 1) * c)
        g_c = g_all[sl]
        beta_c = beta_all[sl]
        cs_f = jnp.dot(tri_f, g_c, preferred_element_type=F32, precision=lax.Precision.HIGHEST)
        cs_b = jnp.dot(tri_b, g_c, preferred_element_type=F32, precision=lax.Precision.HIGHEST)
        gc_all = jnp.where((lane & H_GDN) == 0, cs_f, cs_b)
        gc_t = gc_all.T
        for h in range(H_GDN):
            qc, kc, vc = qs[h][sl], ks[h][sl], vs[h][sl]
            kq = jnp.concatenate([kc, qc], axis=0).astype(BF16)
            kkqk = lax.dot_general(kq, kc.astype(BF16), (((1,), (1,)), ((), ())),
                                   preferred_element_type=F32)
            kk, qk = kkqk[:c], kkqk[c:]
            k_t = kc.T
            for d in range(2):
                gi = d * H_GDN + h
                bi = 2 * H_GDN + gi
                gcol = gc_all[:, gi:gi + 1]
                bcol = beta_c[:, bi:bi + 1]
                grow = gc_t[gi:gi + 1, :]
                incl = (ri >= ci) if d == 0 else (ri <= ci)
                strict = (ri > ci) if d == 0 else (ri < ci)
                dec = jnp.exp(jnp.where(incl, gcol - grow, -jnp.inf))
                a_mat = jnp.where(strict, kk * bcol * dec, 0.0)
                egc = jnp.exp(gcol)
                rhs = jnp.concatenate([vc * bcol, kc * (bcol * egc)], axis=1).astype(BF16)
                glast = gcol[c - 1:c, :] if d == 0 else gcol[0:1, :]
                wq_ref[d, h, n, c:2 * c, :] = (qc * egc).astype(BF16)
                ik_ref[d, h, n, 0:c, :] = (qk * dec).astype(BF16)
                ik_ref[d, h, n, c:c + DK, :] = (k_t * jnp.exp(glast - grow)).astype(BF16)
                dl_ref[d, h, n] = jnp.broadcast_to(jnp.exp(glast), (SUBLANES, LANES))
                chains.append((d, h, n, a_mat, rhs))

    invs = [eye - jnp.where((xr >> 1) == 0, ch[3], 0.0) for ch in chains]
    for lvl in range(1, int(np.log2(c))):
        sel = ((xr >> (lvl + 1)) == 0) & ((xr >> lvl) != 0)
        invb = [inv.astype(BF16) for inv in invs]
        mts = [jnp.dot(jnp.where(sel, ch[3], 0.0).astype(BF16), ib, preferred_element_type=F32)
               for ch, ib in zip(chains, invb)]
        invs = [inv - jnp.dot(ib, mt.astype(BF16), preferred_element_type=F32)
                for inv, ib, mt in zip(invs, invb, mts)]
    for (d, h, n, _, rhs), inv in zip(chains, invs):
        uw = jnp.dot(inv.astype(BF16), rhs, preferred_element_type=F32)
        u_ref[d, h, n * c:(n + 1) * c, :] = uw[:, :DK]
        wq_ref[d, h, n, 0:c, :] = uw[:, DK:].astype(BF16)


def _gdn_prep_call(gd, conv_w, ab, par, *, tb):
    b, t, _ = gd.shape
    c = CHUNK
    nchunk = t // c
    nb8 = t // SUBLANES
    tb8 = tb // SUBLANES
    ncb = tb // c
    in_specs = [pl.BlockSpec((None, tb, GDN_QKV_W), lambda i, j: (i, j, 0)),
                pl.BlockSpec((None, SUBLANES, GDN_QKV_W),
                             lambda i, j: (i, jnp.maximum(j * tb8 - 1, 0), 0)),
                pl.BlockSpec((None, SUBLANES, GDN_QKV_W),
                             lambda i, j: (i, jnp.minimum((j + 1) * tb8, nb8 - 1), 0)),
                pl.BlockSpec((3, GDN_QKV_W), lambda i, j: (0, 0)),
                pl.BlockSpec((None, tb, LANES), lambda i, j: (i, j, 0)),
                pl.BlockSpec((SUBLANES, LANES), lambda i, j: (0, 0))]
    out_specs = [
        pl.BlockSpec((None, 2, H_GDN, ncb, 2 * c, DK), lambda i, j: (i, 0, 0, j, 0, 0)),
        pl.BlockSpec((None, 2, H_GDN, ncb, c + DK, c), lambda i, j: (i, 0, 0, j, 0, 0)),
        pl.BlockSpec((None, 2, H_GDN, tb, DK), lambda i, j: (i, 0, 0, j, 0)),
        pl.BlockSpec((None, 2, H_GDN, ncb, SUBLANES, LANES), lambda i, j: (i, 0, 0, j, 0, 0)),
    ]
    out_shape = [
        jax.ShapeDtypeStruct((b, 2, H_GDN, nchunk, 2 * c, DK), BF16),
        jax.ShapeDtypeStruct((b, 2, H_GDN, nchunk, c + DK, c), BF16),
        jax.ShapeDtypeStruct((b, 2, H_GDN, t, DK), F32),
        jax.ShapeDtypeStruct((b, 2, H_GDN, nchunk, SUBLANES, LANES), F32),
    ]
    return pl.pallas_call(
        functools.partial(_gdn_prep_kernel, tb=tb),
        grid=(b, t // tb),
        in_specs=in_specs,
        out_specs=out_specs,
        out_shape=out_shape,
        compiler_params=_cparams(("parallel", "arbitrary")),
        name="gdn_prep",
    )(gd, gd, gd, conv_w, ab, par)


def _gdn_scan_kernel(wqf_ref, ikf_ref, uf_ref, dlf_ref, wqb_ref, ikb_ref, ub_ref, dlb_ref, s0_ref,
                     *out_refs, nb, with_out):
    c = CHUNK
    if with_out:
        of_ref, ob_ref, sfin_ref, s_ref = out_refs
    else:
        sfin_ref, s_ref = out_refs
    step = pl.program_id(1)

    @pl.when(step == 0)
    def _():
        s_ref[...] = s0_ref[...]

    refs = ((wqf_ref, ikf_ref, uf_ref, dlf_ref), (wqb_ref, ikb_ref, ub_ref, dlb_ref))
    chains = [(d, h) for d in range(2) for h in range(H_GDN)]
    for n in range(nb):
        cns = [n if d == 0 else nb - 1 - n for d, _ in chains]
        ss = [s_ref[d, h] for d, h in chains]
        r1s = [jnp.dot(refs[d][0][h, cn], s.astype(BF16), preferred_element_type=F32)
               for (d, h), cn, s in zip(chains, cns, ss)]
        vns = [refs[d][2][h, cn * c:(cn + 1) * c, :] - r1[:c]
               for (d, h), cn, r1 in zip(chains, cns, r1s)]
        r2s = [jnp.dot(refs[d][1][h, cn], vn.astype(BF16), preferred_element_type=F32)
               for (d, h), cn, vn in zip(chains, cns, vns)]
        for (d, h), cn, s, r1, r2 in zip(chains, cns, ss, r1s, r2s):
            if with_out:
                o_ref = of_ref if d == 0 else ob_ref
                o_ref[cn * c:(cn + 1) * c, h * DK:(h + 1) * DK] = r1[c:] + r2[:c]
            s_ref[d, h] = s * refs[d][3][h, cn, 0:1, 0:1] + r2[c:]

    @pl.when(step == pl.num_programs(1) - 1)
    def _():
        sfin_ref[...] = s_ref[...]


def _gdn_scan_call(wq, ik, u, dl, s0, *, nb, with_out):
    b = wq.shape[0]
    c = CHUNK
    nchunk = wq.shape[3]
    t = nchunk * c
    ns = nchunk // nb

    def spec(shape_tail, d, tok_scale):
        nd = len(shape_tail)
        if d == 0:
            return pl.BlockSpec((None, None, H_GDN, tok_scale) + shape_tail,
                                lambda i, j: (i, 0, 0, j) + (0,) * nd)
        return pl.BlockSpec((None, None, H_GDN, tok_scale) + shape_tail,
                            lambda i, j: (i, 1, 0, ns - 1 - j) + (0,) * nd)

    in_specs = []
    for d in range(2):
        in_specs += [spec((2 * c, DK), d, nb), spec((c + DK, c), d, nb),
                     spec((DK,), d, nb * c), spec((SUBLANES, LANES), d, nb)]
    in_specs.append(pl.BlockSpec((None, 2, H_GDN, DK, DK), lambda i, j: (i, 0, 0, 0, 0)))
    out_specs = []
    out_shape = []
    if with_out:
        out_specs += [pl.BlockSpec((None, nb * c, GDN_W), lambda i, j: (i, j, 0)),
                      pl.BlockSpec((None, nb * c, GDN_W), lambda i, j: (i, ns - 1 - j, 0))]
        out_shape += [jax.ShapeDtypeStruct((b, t, GDN_W), F32)] * 2
    out_specs.append(pl.BlockSpec((None, 2, H_GDN, DK, DK), lambda i, j: (i, 0, 0, 0, 0)))
    out_shape.append(jax.ShapeDtypeStruct((b, 2, H_GDN, DK, DK), F32))
    return pl.pallas_call(
        functools.partial(_gdn_scan_kernel, nb=nb, with_out=with_out),
        grid=(b, ns),
        in_specs=in_specs,
        out_specs=out_specs,
        out_shape=out_shape,
        scratch_shapes=[pltpu.VMEM((2, H_GDN, DK, DK), F32)],
        compiler_params=_cparams(("parallel", "arbitrary")),
        name="gdn_scan" if with_out else "gdn_scan_ctx",
    )(wq, ik, u, dl, wq, ik, u, dl, s0)


def _outproj_kernel(x_ref, oa_ref, of_ref, ob_ref, z_ref, g1_ref, ag_ref, gg_ref, w_ref, o_ref):
    oa = _rms(oa_ref[...], ag_ref[...])
    og = of_ref[...] + ob_ref[...]
    z = z_ref[...]
    gg = gg_ref[...]
    parts = [oa.astype(BF16)]
    for h in range(H_GDN):
        sl = slice(h * DK, (h + 1) * DK)
        parts.append((_rms(og[:, sl], gg) * _silu(z[:, sl])).astype(BF16))
    cat = jnp.concatenate(parts, axis=1)
    lat = jnp.dot(cat, w_ref[...], preferred_element_type=F32)
    o_ref[...] = x_ref[...] + g1_ref[...] * lat


def _outproj_call(x, oa, of, ob, z, g1, ag, gg, w_out, *, tm):
    b, t, d = x.shape
    row = lambda i, j: (i, j, 0)
    const2 = lambda i, j: (0, 0)
    return pl.pallas_call(
        _outproj_kernel,
        grid=(b, t // tm),
        in_specs=[pl.BlockSpec((None, tm, d), row),
                  pl.BlockSpec((None, tm, ATTN_W), row),
                  pl.BlockSpec((None, tm, GDN_W), row),
                  pl.BlockSpec((None, tm, GDN_W), row),
                  pl.BlockSpec((None, tm, GDN_W), row),
                  pl.BlockSpec((None, 1, d), lambda i, j: (i, 0, 0)),
                  pl.BlockSpec((1, ATTN_W), const2),
                  pl.BlockSpec((1, DK), const2),
                  pl.BlockSpec((ATTN_W + GDN_W, d), const2)],
        out_specs=pl.BlockSpec((None, tm, d), row),
        out_shape=jax.ShapeDtypeStruct((b, t, d), F32),
        compiler_params=_cparams(("parallel", "parallel")),
        name="outproj",
    )(x, oa, of, ob, z, g1, ag, gg, w_out)


def _ffn_kernel(x_ref, xp_ref, xn_ref, sh_ref, sc_ref, g2_ref, ng_ref, wu_ref, cw_ref, cb_ref,
                wd_ref, fg_ref, o_ref, *, tm, nc):
    t_idx = pl.program_id(1)
    nt = pl.num_programs(1)
    d_ff = wd_ref.shape[0]
    x = x_ref[...]
    ng, sc, sh = ng_ref[...], sc_ref[...], sh_ref[...]
    h = (_rms(x, ng) * (1.0 + sc) + sh).astype(BF16)
    halo = jnp.concatenate([xp_ref[...], xn_ref[...]], axis=0)
    hh = (_rms(halo, ng) * (1.0 + sc) + sh).astype(BF16)
    row = lax.broadcasted_iota(jnp.int32, (tm, nc), 0)
    has_prev = t_idx > 0
    has_next = t_idx < nt - 1
    acc = jnp.zeros(x.shape, F32)
    for c in range(d_ff // nc):
        us = []
        for base in (c * nc, d_ff + c * nc):
            wsl = wu_ref[:, base:base + nc]
            up = jnp.dot(h, wsl, preferred_element_type=F32)
            uph = jnp.dot(hh, wsl, preferred_element_type=F32)
            prev_row = jnp.where(has_prev, uph[SUBLANES - 1:SUBLANES, :], 0.0)
            next_row = jnp.where(has_next, uph[SUBLANES:SUBLANES + 1, :], 0.0)
            um = jnp.where(row == 0, prev_row, pltpu.roll(up, 1, 0))
            upn = jnp.where(row == tm - 1, next_row, pltpu.roll(up, tm - 1, 0))
            w = cw_ref[:, base:base + nc]
            us.append(um * w[0:1, :] + up * w[1:2, :] + upn * w[2:3, :] + cb_ref[:, base:base + nc])
        act = (_silu(us[0]) * us[1]).astype(BF16)
        acc = acc + jnp.dot(act, wd_ref[c * nc:(c + 1) * nc, :], preferred_element_type=F32)
    x2 = x + g2_ref[...] * acc
    o_ref[...] = _rms(x2, fg_ref[...])


def _ffn_call(x, sh2, sc2, g2, ng, w_up, conv_w, conv_b, w_down, fg, *, tm, nc):
    b, t, d = x.shape
    d_ff = w_down.shape[0]
    nb8 = t // SUBLANES
    tm8 = tm // SUBLANES
    row = lambda i, j: (i, j, 0)
    mod = lambda i, j: (i, 0, 0)
    const2 = lambda i, j: (0, 0)
    return pl.pallas_call(
        functools.partial(_ffn_kernel, tm=tm, nc=nc),
        grid=(b, t // tm),
        in_specs=[pl.BlockSpec((None, tm, d), row),
                  pl.BlockSpec((None, SUBLANES, d), lambda i, j: (i, jnp.maximum(j * tm8 - 1, 0), 0)),
                  pl.BlockSpec((None, SUBLANES, d), lambda i, j: (i, jnp.minimum((j + 1) * tm8, nb8 - 1), 0)),
                  pl.BlockSpec((None, 1, d), mod),
                  pl.BlockSpec((None, 1, d), mod),
                  pl.BlockSpec((None, 1, d), mod),
                  pl.BlockSpec((1, d), const2),
                  pl.BlockSpec((d, 2 * d_ff), const2),
                  pl.BlockSpec((3, 2 * d_ff), const2),
                  pl.BlockSpec((1, 2 * d_ff), const2),
                  pl.BlockSpec((d_ff, d), const2),
                  pl.BlockSpec((1, d), const2)],
        out_specs=pl.BlockSpec((None, tm, d), row),
        out_shape=jax.ShapeDtypeStruct((b, t, d), F32),
        compiler_params=_cparams(("parallel", "parallel")),
        name="ffn",
    )(x, x, x, sh2, sc2, g2, ng, w_up, conv_w, conv_b, w_down, fg)


def _rope_tables(t):
    rows = t // GRID_W
    r = jnp.repeat(jnp.arange(rows, dtype=F32), GRID_W)
    c = jnp.tile(jnp.arange(GRID_W, dtype=F32), rows)
    pairs = HD // 4
    inv_freq = ROPE_THETA ** (-jnp.arange(pairs, dtype=F32) / pairs)
    ang = jnp.concatenate([r[:, None] * inv_freq, c[:, None] * inv_freq], axis=-1)
    cos, sin = jnp.cos(ang), jnp.sin(ang)
    cos128 = jnp.tile(cos, (1, LANES // (HD // 2)))
    sin128 = jnp.tile(jnp.concatenate([-sin, sin], axis=-1), (1, LANES // HD))
    return cos128, sin128


def _pick(n, candidates):
    for c in candidates:
        if n % c == 0:
            return c
    raise ValueError(f"no tile size in {candidates} divides {n}")


def kernel(x, c, ctx, c_ctx, w_mod, b_mod, norm1_g, w_in, q_norm_g, k_norm_g, attn_out_g,
           conv_qkv_w, a_log_f, a_log_b, dt_bias_f, dt_bias_b, gdn_norm_g, w_out, norm2_g,
           w_up, ffn_conv_w, ffn_conv_b, w_down, final_norm_g):
    depth = w_mod.shape[0]
    assert depth == 1, "only the single-layer configuration is implemented"
    b, t, d = x.shape
    tc = ctx.shape[1]
    assert t % CHUNK == 0 and tc % CHUNK == 0 and t % GRID_W == 0

    rows = -(-(b + 1) // SUBLANES) * SUBLANES
    cc = jnp.zeros((rows, d), F32).at[:b].set(c).at[b].set(c_ctx)
    mod = _mod_call(cc, w_mod[0], b_mod[0]).reshape(rows, N_MOD, d)
    lat = [mod[:b, i][:, None, :] for i in range(N_MOD)]
    cm = [mod[b:b + 1, i][:, None, :] for i in range(2)]

    w_pack = jnp.pad(w_in[0], ((0, 0), (0, PACK_W - w_in.shape[-1]))).astype(BF16)
    qg = jnp.tile(q_norm_g[0], LANES // HD)[None, :]
    kg = jnp.tile(k_norm_g[0], LANES // HD)[None, :]
    cos, sin = _rope_tables(t)
    g1n = norm1_g[0][None, :]

    tm = _pick(t, (512, 256, 128))
    q, kl, vl, gdl, z, abl = _inproj_call(x, lat[0], lat[1], g1n, w_pack, qg, kg, cos, sin,
                                          latent=True, tm=tm)
    kc, vc, gdc, abc = _inproj_call(ctx, cm[0], cm[1], g1n, w_pack, None, kg, None, None,
                                    latent=False, tm=_pick(tc, (256, 128)))

    o_attn = _attn_call(q, kc, vc, kl, vl, tq=_pick(t, (256, 128)), tk=_pick(t, (512, 256, 128)))

    par = jnp.zeros((SUBLANES, LANES), F32)
    par = par.at[0, :2 * H_GDN].set(jnp.concatenate([a_log_f[0], a_log_b[0]]))
    par = par.at[1, :2 * H_GDN].set(jnp.concatenate([dt_bias_f[0], dt_bias_b[0]]))
    cw = conv_qkv_w[0]
    prep_c = _gdn_prep_call(gdc, cw, abc, par, tb=_pick(tc, (256, 128)))
    s0 = jnp.zeros((b, 2, H_GDN, DK, DK), F32)
    (s_ctx,) = _gdn_scan_call(*prep_c, s0, nb=tc // CHUNK, with_out=False)
    prep_l = _gdn_prep_call(gdl, cw, abl, par, tb=_pick(t, (256, 128)))
    o_f, o_b, _ = _gdn_scan_call(*prep_l, s_ctx, nb=_pick(t // CHUNK, (4, 2, 1)), with_out=True)

    x1 = _outproj_call(x, o_attn, o_f, o_b, z, lat[2], attn_out_g[0][None, :],
                       gdn_norm_g[0][None, :], w_out[0].astype(BF16), tm=tm)
    return _ffn_call(x1, lat[3], lat[4], lat[5], norm2_g[0][None, :], w_up[0].astype(BF16),
                     ffn_conv_w[0], ffn_conv_b[0][None, :], w_down[0].astype(BF16),
                     final_norm_g[None, :], tm=tm, nc=256)
```

```python
import functools

import jax
import jax.numpy as jnp
import numpy as np
from jax import lax
from jax.experimental import pallas as pl
from jax.experimental.pallas import tpu as pltpu

F32 = jnp.float32
BF16 = jnp.bfloat16

GRID_W = 64
H_ATTN = 8
KV_ATTN = 2
HD = 64
GQA = H_ATTN // KV_ATTN
ATTN_W = H_ATTN * HD
KV_W = KV_ATTN * HD
H_GDN = 4
DK = 128
GDN_QK = H_GDN * DK
GDN_QKV_W = 3 * GDN_QK
GDN_W = H_GDN * DK
N_MOD = 6
ROPE_THETA = 10000.0
EPS = 1e-6
Q_SCALE = HD ** -0.5 * float(np.log2(np.e))

LANES = 128
SUBLANES = 8
VMEM_LIMIT_BYTES = 56 * 1024 * 1024

CHUNK = 128
GDN_GROUP = 2

COL_Q = 0
COL_K = COL_Q + ATTN_W
COL_V = COL_K + KV_W
COL_G = COL_V + KV_W
COL_Z = COL_G + GDN_QKV_W
COL_AB = COL_Z + GDN_W
PACK_W = COL_AB + LANES


def _cparams(sem):
    return pltpu.CompilerParams(dimension_semantics=sem, vmem_limit_bytes=VMEM_LIMIT_BYTES)


def _silu(x):
    return x * jax.nn.sigmoid(x)


def _rms(x, gain):
    return x * lax.rsqrt(jnp.mean(x * x, axis=-1, keepdims=True) + EPS) * gain


def _mod_kernel(c_ref, w_ref, b_ref, o_ref):
    s = _silu(c_ref[...])
    o_ref[...] = jnp.dot(s, w_ref[...], preferred_element_type=F32,
                         precision=lax.Precision.HIGHEST) + b_ref[...]


def _mod_call(cc, w_mod, b_mod):
    rows, d = cc.shape
    n = w_mod.shape[1]
    bn = 1024
    return pl.pallas_call(
        _mod_kernel,
        grid=(n // bn,),
        in_specs=[pl.BlockSpec((rows, d), lambda j: (0, 0)),
                  pl.BlockSpec((d, bn), lambda j: (0, j)),
                  pl.BlockSpec((1, bn), lambda j: (0, j))],
        out_specs=pl.BlockSpec((rows, bn), lambda j: (0, j)),
        out_shape=jax.ShapeDtypeStruct((rows, n), F32),
        compiler_params=_cparams(("arbitrary",)),
        name="mod",
    )(cc, w_mod, b_mod.reshape(1, n))


def _low_half(shape):
    return (lax.broadcasted_iota(jnp.int32, shape, 1) & HD) == 0


def _head_norm(t, gain):
    lo = _low_half(t.shape)
    sq = t * t
    s_lo = jnp.sum(jnp.where(lo, sq, 0.0), axis=-1, keepdims=True)
    s_hi = jnp.sum(jnp.where(lo, 0.0, sq), axis=-1, keepdims=True)
    ms = jnp.where(lo, s_lo, s_hi) * (1.0 / HD)
    return t * lax.rsqrt(ms + EPS) * gain


def _rope(t, cos, sin_signed):
    first = (lax.broadcasted_iota(jnp.int32, t.shape, 1) & (HD // 2)) == 0
    partner = jnp.where(first, pltpu.roll(t, LANES - HD // 2, 1), pltpu.roll(t, HD // 2, 1))
    return t * cos + partner * sin_signed


def _inproj_kernel(*refs, latent):
    if latent:
        (x_ref, sh_ref, sc_ref, g_ref, w_ref, qg_ref, kg_ref, cos_ref, sin_ref,
         q_ref, k_ref, v_ref, gd_ref, z_ref, ab_ref) = refs
    else:
        (x_ref, sh_ref, sc_ref, g_ref, w_ref, kg_ref,
         k_ref, v_ref, gd_ref, ab_ref) = refs
    x = x_ref[...]
    h = _rms(x, g_ref[...]) * (1.0 + sc_ref[...]) + sh_ref[...]
    y = jnp.dot(h.astype(BF16), w_ref[...], preferred_element_type=F32)

    kt = _head_norm(y[:, COL_K:COL_K + KV_W], kg_ref[...])
    if latent:
        cos = cos_ref[...]
        sin = sin_ref[...]
        kt = _rope(kt, cos, sin)
        for i in range(ATTN_W // LANES):
            qt = y[:, COL_Q + i * LANES:COL_Q + (i + 1) * LANES]
            qt = _rope(_head_norm(qt, qg_ref[...]), cos, sin) * Q_SCALE
            q_ref[:, i * LANES:(i + 1) * LANES] = qt.astype(BF16)
        z_ref[...] = y[:, COL_Z:COL_Z + GDN_W]
    k_ref[...] = kt.astype(BF16)
    v_ref[...] = y[:, COL_V:COL_V + KV_W].astype(BF16)
    gd_ref[...] = y[:, COL_G:COL_G + GDN_QKV_W]
    ab_ref[...] = y[:, COL_AB:COL_AB + LANES]


def _inproj_call(x, shift, scale, gain, w_pack, qg, kg, cos, sin, *, latent, tm):
    b, t, d = x.shape
    nt = t // tm
    per_batch = shift.shape[0] == b
    mod_map = (lambda i, j: (i, 0, 0)) if per_batch else (lambda i, j: (0, 0, 0))
    row = lambda i, j: (i, j, 0)
    const2 = lambda i, j: (0, 0)
    in_specs = [pl.BlockSpec((None, tm, d), row),
                pl.BlockSpec((None, 1, d), mod_map),
                pl.BlockSpec((None, 1, d), mod_map),
                pl.BlockSpec((1, d), const2),
                pl.BlockSpec((d, PACK_W), const2)]
    args = [x, shift, scale, gain, w_pack]
    if latent:
        in_specs += [pl.BlockSpec((1, LANES), const2), pl.BlockSpec((1, LANES), const2),
                     pl.BlockSpec((tm, LANES), lambda i, j: (j, 0)),
                     pl.BlockSpec((tm, LANES), lambda i, j: (j, 0))]
        args += [qg, kg, cos, sin]
    else:
        in_specs += [pl.BlockSpec((1, LANES), const2)]
        args += [kg]
    kv_spec = pl.BlockSpec((None, tm, KV_W), row)
    kv_shape = jax.ShapeDtypeStruct((b, t, KV_W), BF16)
    out_specs = [kv_spec, kv_spec, pl.BlockSpec((None, tm, GDN_QKV_W), row)]
    out_shape = [kv_shape, kv_shape, jax.ShapeDtypeStruct((b, t, GDN_QKV_W), F32)]
    if latent:
        out_specs = [pl.BlockSpec((None, tm, ATTN_W), row)] + out_specs + [pl.BlockSpec((None, tm, GDN_W), row)]
        out_shape = [jax.ShapeDtypeStruct((b, t, ATTN_W), BF16)] + out_shape + [jax.ShapeDtypeStruct((b, t, GDN_W), F32)]
    out_specs.append(pl.BlockSpec((None, tm, LANES), row))
    out_shape.append(jax.ShapeDtypeStruct((b, t, LANES), F32))
    return pl.pallas_call(
        functools.partial(_inproj_kernel, latent=latent),
        grid=(b, nt),
        in_specs=in_specs,
        out_specs=out_specs,
        out_shape=out_shape,
        compiler_params=_cparams(("parallel", "parallel")),
        name="inproj_lat" if latent else "inproj_ctx",
    )(*args)


def _attn_kernel(q_ref, kc_ref, vc_ref, kl_ref, vl_ref, o_ref, vt_ref, *, tq, tk):
    j = pl.program_id(1)
    tc = kc_ref.shape[0]
    n_lat = kl_ref.shape[0]

    @pl.when(pl.program_id(2) == 0)
    def _():
        vt_ref[:, 0:tc] = vc_ref[...].astype(F32).T.astype(BF16)
        for c in range(n_lat // tk):
            vt_ref[:, tc + c * tk:tc + (c + 1) * tk] = (
                vl_ref[c * tk:(c + 1) * tk, :].astype(F32).T.astype(BF16))

    qt = q_ref[...].astype(F32).T
    parts = []
    for g in range(GQA):
        qg = qt[g * HD:(g + 1) * HD, :]
        parts.append(jnp.concatenate([jnp.where(j == 0, qg, 0.0), jnp.where(j == 1, qg, 0.0)], axis=0))
    wq = jnp.concatenate(parts, axis=1).astype(BF16)
    nq = wq.shape[1]

    segs = [(kc_ref, 0, 0, tc)] + [(kl_ref, c * tk, tc + c * tk, tk) for c in range(n_lat // tk)]

    def scores(seg):
        k_ref, start, _, size = seg
        return jnp.dot(k_ref[start:start + size, :], wq, preferred_element_type=F32)

    m = jnp.full((1, nq), -jnp.inf, F32)
    l = jnp.zeros((1, nq), F32)
    acc = jnp.zeros((LANES, nq), F32)
    s_next = scores(segs[0])
    for idx, (_, _, voff, size) in enumerate(segs):
        s = s_next
        if idx + 1 < len(segs):
            s_next = scores(segs[idx + 1])
        m_new = jnp.maximum(m, jnp.max(s, axis=0, keepdims=True))
        alpha = jnp.exp2(m - m_new)
        p = jnp.exp2(s - m_new)
        l = alpha * l + jnp.sum(p, axis=0, keepdims=True)
        acc = alpha * acc + jnp.dot(vt_ref[:, voff:voff + size], p.astype(BF16),
                                    preferred_element_type=F32)
        m = m_new
    o2 = acc / l
    ot = jnp.where(j == 0, o2[:HD], o2[HD:])
    og = jnp.concatenate([ot[:, g * tq:(g + 1) * tq] for g in range(GQA)], axis=0)
    o_ref[...] = og.T


def _attn_call(q, kc, vc, kl, vl, *, tq, tk):
    b, t, _ = q.shape
    tc = kc.shape[1]
    gw = GQA * HD
    kv_c = pl.BlockSpec((None, tc, KV_W), lambda i, j, n: (i, 0, 0))
    kv_l = pl.BlockSpec((None, t, KV_W), lambda i, j, n: (i, 0, 0))
    qo = pl.BlockSpec((None, tq, gw), lambda i, j, n: (i, n, j))
    return pl.pallas_call(
        functools.partial(_attn_kernel, tq=tq, tk=tk),
        grid=(b, KV_ATTN, t // tq),
        in_specs=[qo, kv_c, kv_c, kv_l, kv_l],
        out_specs=qo,
        out_shape=jax.ShapeDtypeStruct((b, t, ATTN_W), F32),
        scratch_shapes=[pltpu.VMEM((KV_W, tc + t), BF16)],
        compiler_params=_cparams(("parallel", "arbitrary", "arbitrary")),
        name="attn",
    )(q, kc, vc, kl, vl)


def _gdn_prep_kernel(x_ref, xp_ref, xn_ref, cw_ref, ab_ref, par_ref,
                     wq_ref, ik_ref, u_ref, dl_ref, *, tb):
    c = CHUNK
    t_idx = pl.program_id(1)
    nt = pl.num_programs(1)

    x = x_ref[...]
    row = lax.broadcasted_iota(jnp.int32, x.shape, 0)
    prev_row = jnp.where(t_idx > 0, xp_ref[SUBLANES - 1:SUBLANES, :], 0.0)
    next_row = jnp.where(t_idx < nt - 1, xn_ref[0:1, :], 0.0)
    xm = jnp.where(row == 0, prev_row, pltpu.roll(x, 1, 0))
    xp = jnp.where(row == tb - 1, next_row, pltpu.roll(x, tb - 1, 0))
    w = cw_ref[...]

    def l2n(a):
        return a * lax.rsqrt(jnp.sum(a * a, axis=-1, keepdims=True) + EPS)

    ab = ab_ref[...]
    par = par_ref[...]
    g_all = -jnp.exp(par[0:1, :]) * jax.nn.softplus(ab + par[1:2, :])
    beta_all = jax.nn.sigmoid(ab)

    ri = lax.broadcasted_iota(jnp.int32, (c, c), 0)
    ci = lax.broadcasted_iota(jnp.int32, (c, c), 1)
    eye = (ri == ci).astype(F32)
    xr = ri ^ ci
    lane = lax.broadcasted_iota(jnp.int32, (c, LANES), 1)
    tri_f = (ri >= ci).astype(F32)
    tri_b = (ri <= ci).astype(F32)

    def blockdiag(xy):
        z = jnp.zeros((c, c), xy.dtype)
        return jnp.concatenate([jnp.concatenate([xy[:, :c], z], axis=1),
                                jnp.concatenate([z, xy[:, c:]], axis=1)], axis=0)

    def prep_chunk(n, chains):
        sl = slice(n * c, (n + 1) * c)
        y = _silu(xm[sl] * w[0:1, :] + x[sl] * w[1:2, :] + xp[sl] * w[2:3, :])
        qs = [l2n(y[:, h * DK:(h + 1) * DK]) * (DK ** -0.5) for h in range(H_GDN)]
        ks = [l2n(y[:, GDN_QK + h * DK:GDN_QK + (h + 1) * DK]) for h in range(H_GDN)]
        vs = [y[:, 2 * GDN_QK + h * DK:2 * GDN_QK + (h + 1) * DK] for h in range(H_GDN)]
        yield
        g_c = g_all[sl]
        beta_c = beta_all[sl]
        cs_f = jnp.dot(tri_f, g_c, preferred_element_type=F32, precision=lax.Precision.HIGHEST)
        cs_b = jnp.dot(tri_b, g_c, preferred_element_type=F32, precision=lax.Precision.HIGHEST)
        gc_all = jnp.where((lane & H_GDN) == 0, cs_f, cs_b)
        gc_t = gc_all.T
        kkqk_h = []
        for h0 in range(0, H_GDN, 2):
            kq2 = jnp.concatenate(
                [jnp.concatenate([ks[h], qs[h]], axis=0) for h in (h0, h0 + 1)],
                axis=1).astype(BF16)
            k2 = blockdiag(jnp.concatenate([ks[h0], ks[h0 + 1]], axis=1).astype(BF16))
            res = lax.dot_general(kq2, k2, (((1,), (1,)), ((), ())), preferred_element_type=F32)
            kkqk_h += [res[:, :c], res[:, c:]]
        yield
        for h in range(H_GDN):
            qc, kc, vc = qs[h], ks[h], vs[h]
            kk, qk = kkqk_h[h][:c], kkqk_h[h][c:]
            k_t = kc.T
            for d in range(2):
                gi = d * H_GDN + h
                bi = 2 * H_GDN + gi
                gcol = gc_all[:, gi:gi + 1]
                bcol = beta_c[:, bi:bi + 1]
                grow = gc_t[gi:gi + 1, :]
                incl = (ri >= ci) if d == 0 else (ri <= ci)
                strict = (ri > ci) if d == 0 else (ri < ci)
                dec = jnp.exp(jnp.where(incl, gcol - grow, -jnp.inf))
                a_mat = jnp.where(strict, kk * bcol * dec, 0.0)
                egc = jnp.exp(gcol)
                rhs = jnp.concatenate([vc * bcol, kc * (bcol * egc)], axis=1).astype(BF16)
                glast = gcol[c - 1:c, :] if d == 0 else gcol[0:1, :]
                wq_ref[d, h, n, c:2 * c, :] = (qc * egc).astype(BF16)
                ik_ref[d, h, n, 0:c, :] = (qk * dec).astype(BF16)
                ik_ref[d, h, n, c:c + DK, :] = (k_t * jnp.exp(glast - grow)).astype(BF16)
                dl_ref[d, h, n] = jnp.broadcast_to(jnp.exp(glast), (SUBLANES, LANES))
                chains.append((d, h, n, a_mat, rhs))
            yield

    xr2 = jnp.concatenate([xr, xr], axis=1)
    eye2 = jnp.concatenate([eye, eye], axis=1)

    def solve(chains):
        a_pairs = [jnp.concatenate([chains[i][3], chains[i + 1][3]], axis=1)
                   for i in range(0, len(chains), 2)]
        inv_pairs = [eye2 - jnp.where((xr2 >> 1) == 0, ap, 0.0) for ap in a_pairs]
        for lvl in range(1, int(np.log2(c))):
            sel = ((xr2 >> (lvl + 1)) == 0) & ((xr2 >> lvl) != 0)
            invb = [ip.astype(BF16) for ip in inv_pairs]
            mts = [jnp.dot(jnp.where(sel, ap, 0.0).astype(BF16), blockdiag(ib),
                           preferred_element_type=F32) for ap, ib in zip(a_pairs, invb)]
            yield
            inv_pairs = [ip - jnp.dot(ib, blockdiag(mt.astype(BF16)), preferred_element_type=F32)
                         for ip, ib, mt in zip(inv_pairs, invb, mts)]
            yield
        invs = [ip[:, s * c:(s + 1) * c] for ip in inv_pairs for s in range(2)]
        for (d, h, n, _, rhs), inv in zip(chains, invs):
            uw = jnp.dot(inv.astype(BF16), rhs, preferred_element_type=F32)
            u_ref[d, h, n * c:(n + 1) * c, :] = uw[:, :DK]
            wq_ref[d, h, n, 0:c, :] = uw[:, DK:].astype(BF16)

    def prep_group(chunk_ids, chains):
        for n in chunk_ids:
            yield from prep_chunk(n, chains)

    n_chunks = tb // c
    group = min(GDN_GROUP, n_chunks)
    pending = []
    for _ in prep_group(range(group), pending):
        pass
    for g0 in range(group, n_chunks + group, group):
        current, pending = pending, []
        nxt = prep_group(range(g0, min(g0 + group, n_chunks)), pending)
        for _ in solve(current):
            next(nxt, None)
        for _ in nxt:
            pass


def _gdn_prep_call(gd, conv_w, ab, par, *, tb):
    b, t, _ = gd.shape
    c = CHUNK
    nchunk = t // c
    nb8 = t // SUBLANES
    tb8 = tb // SUBLANES
    ncb = tb // c
    in_specs = [pl.BlockSpec((None, tb, GDN_QKV_W), lambda i, j: (i, j, 0)),
                pl.BlockSpec((None, SUBLANES, GDN_QKV_W),
                             lambda i, j: (i, jnp.maximum(j * tb8 - 1, 0), 0)),
                pl.BlockSpec((None, SUBLANES, GDN_QKV_W),
                             lambda i, j: (i, jnp.minimum((j + 1) * tb8, nb8 - 1), 0)),
                pl.BlockSpec((3, GDN_QKV_W), lambda i, j: (0, 0)),
                pl.BlockSpec((None, tb, LANES), lambda i, j: (i, j, 0)),
                pl.BlockSpec((SUBLANES, LANES), lambda i, j: (0, 0))]
    out_specs = [
        pl.BlockSpec((None, 2, H_GDN, ncb, 2 * c, DK), lambda i, j: (i, 0, 0, j, 0, 0)),
        pl.BlockSpec((None, 2, H_GDN, ncb, c + DK, c), lambda i, j: (i, 0, 0, j, 0, 0)),
        pl.BlockSpec((None, 2, H_GDN, tb, DK), lambda i, j: (i, 0, 0, j, 0)),
        pl.BlockSpec((None, 2, H_GDN, ncb, SUBLANES, LANES), lambda i, j: (i, 0, 0, j, 0, 0)),
    ]
    out_shape = [
        jax.ShapeDtypeStruct((b, 2, H_GDN, nchunk, 2 * c, DK), BF16),
        jax.ShapeDtypeStruct((b, 2, H_GDN, nchunk, c + DK, c), BF16),
        jax.ShapeDtypeStruct((b, 2, H_GDN, t, DK), F32),
        jax.ShapeDtypeStruct((b, 2, H_GDN, nchunk, SUBLANES, LANES), F32),
    ]
    return pl.pallas_call(
        functools.partial(_gdn_prep_kernel, tb=tb),
        grid=(b, t // tb),
        in_specs=in_specs,
        out_specs=out_specs,
        out_shape=out_shape,
        compiler_params=_cparams(("parallel", "arbitrary")),
        name="gdn_prep",
    )(gd, gd, gd, conv_w, ab, par)


def _gdn_scan_kernel(wqf_ref, ikf_ref, uf_ref, dlf_ref, wqb_ref, ikb_ref, ub_ref, dlb_ref, s0_ref,
                     *out_refs, nb, with_out):
    c = CHUNK
    if with_out:
        of_ref, ob_ref, sfin_ref, s_ref = out_refs
    else:
        sfin_ref, s_ref = out_refs
    step = pl.program_id(1)

    @pl.when(step == 0)
    def _():
        s_ref[...] = s0_ref[...]

    refs = ((wqf_ref, ikf_ref, uf_ref, dlf_ref), (wqb_ref, ikb_ref, ub_ref, dlb_ref))
    chains = [(d, h) for d in range(2) for h in range(H_GDN)]
    for n in range(nb):
        cns = [n if d == 0 else nb - 1 - n for d, _ in chains]
        ss = [s_ref[d, h] for d, h in chains]
        r1s = [jnp.dot(refs[d][0][h, cn], s.astype(BF16), preferred_element_type=F32)
               for (d, h), cn, s in zip(chains, cns, ss)]
        vns = [refs[d][2][h, cn * c:(cn + 1) * c, :] - r1[:c]
               for (d, h), cn, r1 in zip(chains, cns, r1s)]
        r2s = [jnp.dot(refs[d][1][h, cn], vn.astype(BF16), preferred_element_type=F32)
               for (d, h), cn, vn in zip(chains, cns, vns)]
        for (d, h), cn, s, r1, r2 in zip(chains, cns, ss, r1s, r2s):
            if with_out:
                o_ref = of_ref if d == 0 else ob_ref
                o_ref[cn * c:(cn + 1) * c, h * DK:(h + 1) * DK] = r1[c:] + r2[:c]
            s_ref[d, h] = s * refs[d][3][h, cn, 0:1, 0:1] + r2[c:]

    @pl.when(step == pl.num_programs(1) - 1)
    def _():
        sfin_ref[...] = s_ref[...]


def _gdn_scan_call(wq, ik, u, dl, s0, *, nb, with_out):
    b = wq.shape[0]
    c = CHUNK
    nchunk = wq.shape[3]
    t = nchunk * c
    ns = nchunk // nb

    def spec(shape_tail, d, tok_scale):
        nd = len(shape_tail)
        if d == 0:
            return pl.BlockSpec((None, None, H_GDN, tok_scale) + shape_tail,
                                lambda i, j: (i, 0, 0, j) + (0,) * nd)
        return pl.BlockSpec((None, None, H_GDN, tok_scale) + shape_tail,
                            lambda i, j: (i, 1, 0, ns - 1 - j) + (0,) * nd)

    in_specs = []
    for d in range(2):
        in_specs += [spec((2 * c, DK), d, nb), spec((c + DK, c), d, nb),
                     spec((DK,), d, nb * c), spec((SUBLANES, LANES), d, nb)]
    in_specs.append(pl.BlockSpec((None, 2, H_GDN, DK, DK), lambda i, j: (i, 0, 0, 0, 0)))
    out_specs = []
    out_shape = []
    if with_out:
        out_specs += [pl.BlockSpec((None, nb * c, GDN_W), lambda i, j: (i, j, 0)),
                      pl.BlockSpec((None, nb * c, GDN_W), lambda i, j: (i, ns - 1 - j, 0))]
        out_shape += [jax.ShapeDtypeStruct((b, t, GDN_W), F32)] * 2
    out_specs.append(pl.BlockSpec((None, 2, H_GDN, DK, DK), lambda i, j: (i, 0, 0, 0, 0)))
    out_shape.append(jax.ShapeDtypeStruct((b, 2, H_GDN, DK, DK), F32))
    return pl.pallas_call(
        functools.partial(_gdn_scan_kernel, nb=nb, with_out=with_out),
        grid=(b, ns),
        in_specs=in_specs,
        out_specs=out_specs,
        out_shape=out_shape,
        scratch_shapes=[pltpu.VMEM((2, H_GDN, DK, DK), F32)],
        compiler_params=_cparams(("parallel", "arbitrary")),
        name="gdn_scan" if with_out else "gdn_scan_ctx",
    )(wq, ik, u, dl, wq, ik, u, dl, s0)


def _outproj_kernel(x_ref, oa_ref, of_ref, ob_ref, z_ref, g1_ref, ag_ref, gg_ref, w_ref, o_ref):
    oa = _rms(oa_ref[...], ag_ref[...])
    og = of_ref[...] + ob_ref[...]
    z = z_ref[...]
    gg = gg_ref[...]
    parts = [oa.astype(BF16)]
    for h in range(H_GDN):
        sl = slice(h * DK, (h + 1) * DK)
        parts.append((_rms(og[:, sl], gg) * _silu(z[:, sl])).astype(BF16))
    cat = jnp.concatenate(parts, axis=1)
    lat = jnp.dot(cat, w_ref[...], preferred_element_type=F32)
    o_ref[...] = x_ref[...] + g1_ref[...] * lat


def _outproj_call(x, oa, of, ob, z, g1, ag, gg, w_out, *, tm):
    b, t, d = x.shape
    row = lambda i, j: (i, j, 0)
    const2 = lambda i, j: (0, 0)
    return pl.pallas_call(
        _outproj_kernel,
        grid=(b, t // tm),
        in_specs=[pl.BlockSpec((None, tm, d), row),
                  pl.BlockSpec((None, tm, ATTN_W), row),
                  pl.BlockSpec((None, tm, GDN_W), row),
                  pl.BlockSpec((None, tm, GDN_W), row),
                  pl.BlockSpec((None, tm, GDN_W), row),
                  pl.BlockSpec((None, 1, d), lambda i, j: (i, 0, 0)),
                  pl.BlockSpec((1, ATTN_W), const2),
                  pl.BlockSpec((1, DK), const2),
                  pl.BlockSpec((ATTN_W + GDN_W, d), const2)],
        out_specs=pl.BlockSpec((None, tm, d), row),
        out_shape=jax.ShapeDtypeStruct((b, t, d), F32),
        compiler_params=_cparams(("parallel", "parallel")),
        name="outproj",
    )(x, oa, of, ob, z, g1, ag, gg, w_out)


def _ffn_kernel(x_ref, xp_ref, xn_ref, sh_ref, sc_ref, g2_ref, ng_ref, wu_ref, cw_ref, cb_ref,
                wd_ref, fg_ref, o_ref, *, tm, nc):
    t_idx = pl.program_id(1)
    nt = pl.num_programs(1)
    d_ff = wd_ref.shape[0]
    x = x_ref[...]
    ng, sc, sh = ng_ref[...], sc_ref[...], sh_ref[...]
    xa = jnp.concatenate([x, xp_ref[...], xn_ref[...]], axis=0)
    h = (_rms(xa, ng) * (1.0 + sc) + sh).astype(BF16)
    row8 = lax.broadcasted_iota(jnp.int32, (SUBLANES, nc), 0)
    has_prev = t_idx > 0
    has_next = t_idx < nt - 1

    def up_proj(c):
        return [jnp.dot(h, wu_ref[:, base:base + nc], preferred_element_type=F32)
                for base in (c * nc, d_ff + c * nc)]

    def conv_act(c, ups):
        us = []
        for base, ua in zip((c * nc, d_ff + c * nc), ups):
            up = ua[:tm]
            prev_row = jnp.where(has_prev, ua[tm + SUBLANES - 1:tm + SUBLANES, :], 0.0)
            next_row = jnp.where(has_next, ua[tm + SUBLANES:tm + SUBLANES + 1, :], 0.0)
            dn = pltpu.roll(up, 1, 0)
            um = jnp.concatenate([jnp.where(row8 == 0, prev_row, dn[:SUBLANES]), dn[SUBLANES:]], axis=0)
            nx = pltpu.roll(up, tm - 1, 0)
            upn = jnp.concatenate([nx[:tm - SUBLANES],
                                   jnp.where(row8 == SUBLANES - 1, next_row, nx[tm - SUBLANES:])], axis=0)
            w = cw_ref[:, base:base + nc]
            us.append(um * w[0:1, :] + up * w[1:2, :] + upn * w[2:3, :] + cb_ref[:, base:base + nc])
        return (_silu(us[0]) * us[1]).astype(BF16)

    n_chunks = d_ff // nc
    acts = []
    ups = up_proj(0)
    for c in range(n_chunks):
        ups_next = up_proj(c + 1) if c + 1 < n_chunks else None
        acts.append(conv_act(c, ups))
        ups = ups_next
    acc = jnp.dot(jnp.concatenate(acts, axis=1), wd_ref[...], preferred_element_type=F32)
    x2 = x + g2_ref[...] * acc
    o_ref[...] = _rms(x2, fg_ref[...])


def _ffn_call(x, sh2, sc2, g2, ng, w_up, conv_w, conv_b, w_down, fg, *, tm, nc):
    b, t, d = x.shape
    d_ff = w_down.shape[0]
    nb8 = t // SUBLANES
    tm8 = tm // SUBLANES
    row = lambda i, j: (i, j, 0)
    mod = lambda i, j: (i, 0, 0)
    const2 = lambda i, j: (0, 0)
    return pl.pallas_call(
        functools.partial(_ffn_kernel, tm=tm, nc=nc),
        grid=(b, t // tm),
        in_specs=[pl.BlockSpec((None, tm, d), row),
                  pl.BlockSpec((None, SUBLANES, d), lambda i, j: (i, jnp.maximum(j * tm8 - 1, 0), 0)),
                  pl.BlockSpec((None, SUBLANES, d), lambda i, j: (i, jnp.minimum((j + 1) * tm8, nb8 - 1), 0)),
                  pl.BlockSpec((None, 1, d), mod),
                  pl.BlockSpec((None, 1, d), mod),
                  pl.BlockSpec((None, 1, d), mod),
                  pl.BlockSpec((1, d), const2),
                  pl.BlockSpec((d, 2 * d_ff), const2),
                  pl.BlockSpec((3, 2 * d_ff), const2),
                  pl.BlockSpec((1, 2 * d_ff), const2),
                  pl.BlockSpec((d_ff, d), const2),
                  pl.BlockSpec((1, d), const2)],
        out_specs=pl.BlockSpec((None, tm, d), row),
        out_shape=jax.ShapeDtypeStruct((b, t, d), F32),
        compiler_params=_cparams(("parallel", "parallel")),
        name="ffn",
    )(x, x, x, sh2, sc2, g2, ng, w_up, conv_w, conv_b, w_down, fg)


def _rope_tables(t):
    rows = t // GRID_W
    r = jnp.repeat(jnp.arange(rows, dtype=F32), GRID_W)
    c = jnp.tile(jnp.arange(GRID_W, dtype=F32), rows)
    pairs = HD // 4
    inv_freq = ROPE_THETA ** (-jnp.arange(pairs, dtype=F32) / pairs)
    ang = jnp.concatenate([r[:, None] * inv_freq, c[:, None] * inv_freq], axis=-1)
    cos, sin = jnp.cos(ang), jnp.sin(ang)
    cos128 = jnp.tile(cos, (1, LANES // (HD // 2)))
    sin128 = jnp.tile(jnp.concatenate([-sin, sin], axis=-1), (1, LANES // HD))
    return cos128, sin128


def _pick(n, candidates):
    for c in candidates:
        if n % c == 0:
            return c
    raise ValueError(f"no tile size in {candidates} divides {n}")


def kernel(x, c, ctx, c_ctx, w_mod, b_mod, norm1_g, w_in, q_norm_g, k_norm_g, attn_out_g,
           conv_qkv_w, a_log_f, a_log_b, dt_bias_f, dt_bias_b, gdn_norm_g, w_out, norm2_g,
           w_up, ffn_conv_w, ffn_conv_b, w_down, final_norm_g):
    depth = w_mod.shape[0]
    assert depth == 1, "only the single-layer configuration is implemented"
    b, t, d = x.shape
    tc = ctx.shape[1]
    assert t % CHUNK == 0 and tc % CHUNK == 0 and t % GRID_W == 0

    rows = -(-(b + 1) // SUBLANES) * SUBLANES
    cc = jnp.zeros((rows, d), F32).at[:b].set(c).at[b].set(c_ctx)
    mod = _mod_call(cc, w_mod[0], b_mod[0]).reshape(rows, N_MOD, d)
    lat = [mod[:b, i][:, None, :] for i in range(N_MOD)]
    cm = [mod[b:b + 1, i][:, None, :] for i in range(2)]

    w_pack = jnp.pad(w_in[0], ((0, 0), (0, PACK_W - w_in.shape[-1]))).astype(BF16)
    qg = jnp.tile(q_norm_g[0], LANES // HD)[None, :]
    kg = jnp.tile(k_norm_g[0], LANES // HD)[None, :]
    cos, sin = _rope_tables(t)
    g1n = norm1_g[0][None, :]

    tm = _pick(t, (512, 256, 128))
    q, kl, vl, gdl, z, abl = _inproj_call(x, lat[0], lat[1], g1n, w_pack, qg, kg, cos, sin,
                                          latent=True, tm=tm)
    kc, vc, gdc, abc = _inproj_call(ctx, cm[0], cm[1], g1n, w_pack, None, kg, None, None,
                                    latent=False, tm=_pick(tc, (256, 128)))

    o_attn = _attn_call(q, kc, vc, kl, vl, tq=_pick(t, (512, 256, 128)), tk=_pick(t, (512, 256, 128)))

    par = jnp.zeros((SUBLANES, LANES), F32)
    par = par.at[0, :2 * H_GDN].set(jnp.concatenate([a_log_f[0], a_log_b[0]]))
    par = par.at[1, :2 * H_GDN].set(jnp.concatenate([dt_bias_f[0], dt_bias_b[0]]))
    cw = conv_qkv_w[0]
    prep_c = _gdn_prep_call(gdc, cw, abc, par, tb=_pick(tc, (256, 128)))
    s0 = jnp.zeros((b, 2, H_GDN, DK, DK), F32)
    (s_ctx,) = _gdn_scan_call(*prep_c, s0, nb=tc // CHUNK, with_out=False)
    prep_l = _gdn_prep_call(gdl, cw, abl, par, tb=_pick(t, (512, 256, 128)))
    o_f, o_b, _ = _gdn_scan_call(*prep_l, s_ctx, nb=_pick(t // CHUNK, (4, 2, 1)), with_out=True)

    x1 = _outproj_call(x, o_attn, o_f, o_b, z, lat[2], attn_out_g[0][None, :],
                       gdn_norm_g[0][None, :], w_out[0].astype(BF16), tm=tm)
    return _ffn_call(x1, lat[3], lat[4], lat[5], norm2_g[0][None, :], w_up[0].astype(BF16),
                     ffn_conv_w[0], ffn_conv_b[0][None, :], w_down[0].astype(BF16),
                     final_norm_g[None, :], tm=tm, nc=256)
```

```python
import functools

import jax
import jax.numpy as jnp
import numpy as np
from jax import lax
from jax.experimental import pallas as pl
from jax.experimental.pallas import tpu as pltpu

F32 = jnp.float32
BF16 = jnp.bfloat16

GRID_W = 64
H_ATTN = 8
KV_ATTN = 2
HD = 64
GQA = H_ATTN // KV_ATTN
ATTN_W = H_ATTN * HD
KV_W = KV_ATTN * HD
H_GDN = 4
DK = 128
GDN_QK = H_GDN * DK
GDN_QKV_W = 3 * GDN_QK
GDN_W = H_GDN * DK
N_MOD = 6
ROPE_THETA = 10000.0
EPS = 1e-6
Q_SCALE = HD ** -0.5 * float(np.log2(np.e))

LANES = 128
SUBLANES = 8
HALO = 16
VMEM_LIMIT_BYTES = 56 * 1024 * 1024

CHUNK = 128
GDN_GROUP = 2

COL_Q = 0
COL_K = COL_Q + ATTN_W
COL_V = COL_K + KV_W
COL_G = COL_V + KV_W
COL_Z = COL_G + GDN_QKV_W
COL_AB = COL_Z + GDN_W
PACK_W = COL_AB + LANES


def _cparams(sem):
    return pltpu.CompilerParams(dimension_semantics=sem, vmem_limit_bytes=VMEM_LIMIT_BYTES)


def _silu(x):
    return x * jax.nn.sigmoid(x)


def _rms(x, gain):
    return x * lax.rsqrt(jnp.mean(x * x, axis=-1, keepdims=True) + EPS) * gain


def _mod_kernel(c_ref, w_ref, b_ref, o_ref):
    s = _silu(c_ref[...])
    o_ref[...] = jnp.dot(s, w_ref[...], preferred_element_type=F32,
                         precision=lax.Precision.HIGHEST) + b_ref[...]


def _mod_call(cc, w_mod, b_mod):
    rows, d = cc.shape
    n = w_mod.shape[1]
    bn = 1024
    return pl.pallas_call(
        _mod_kernel,
        grid=(n // bn,),
        in_specs=[pl.BlockSpec((rows, d), lambda j: (0, 0)),
                  pl.BlockSpec((d, bn), lambda j: (0, j)),
                  pl.BlockSpec((1, bn), lambda j: (0, j))],
        out_specs=pl.BlockSpec((rows, bn), lambda j: (0, j)),
        out_shape=jax.ShapeDtypeStruct((rows, n), F32),
        compiler_params=_cparams(("arbitrary",)),
        name="mod",
    )(cc, w_mod, b_mod.reshape(1, n))


def _low_half(shape):
    return (lax.broadcasted_iota(jnp.int32, shape, 1) & HD) == 0


def _head_norm(t, gain):
    lo = _low_half(t.shape)
    sq = t * t
    s_lo = jnp.sum(jnp.where(lo, sq, 0.0), axis=-1, keepdims=True)
    s_hi = jnp.sum(jnp.where(lo, 0.0, sq), axis=-1, keepdims=True)
    ms = jnp.where(lo, s_lo, s_hi) * (1.0 / HD)
    return t * lax.rsqrt(ms + EPS) * gain


def _rope(t, cos, sin_signed):
    first = (lax.broadcasted_iota(jnp.int32, t.shape, 1) & (HD // 2)) == 0
    partner = jnp.where(first, pltpu.roll(t, LANES - HD // 2, 1), pltpu.roll(t, HD // 2, 1))
    return t * cos + partner * sin_signed


def _inproj_kernel(*refs, latent):
    if latent:
        (x_ref, sh_ref, sc_ref, g_ref, w_ref, qg_ref, kg_ref, cos_ref, sin_ref,
         q_ref, k_ref, v_ref, gd_ref, z_ref, ab_ref) = refs
    else:
        (x_ref, sh_ref, sc_ref, g_ref, w_ref, kg_ref,
         k_ref, v_ref, gd_ref, ab_ref) = refs
    x = x_ref[...]
    h = _rms(x, g_ref[...]) * (1.0 + sc_ref[...]) + sh_ref[...]
    y = jnp.dot(h.astype(BF16), w_ref[...], preferred_element_type=F32)

    kt = _head_norm(y[:, COL_K:COL_K + KV_W], kg_ref[...])
    if latent:
        cos = cos_ref[...]
        sin = sin_ref[...]
        kt = _rope(kt, cos, sin)
        for i in range(ATTN_W // LANES):
            qt = y[:, COL_Q + i * LANES:COL_Q + (i + 1) * LANES]
            qt = _rope(_head_norm(qt, qg_ref[...]), cos, sin) * Q_SCALE
            q_ref[:, i * LANES:(i + 1) * LANES] = qt.astype(BF16)
        z_ref[...] = y[:, COL_Z:COL_Z + GDN_W].astype(BF16)
    k_ref[...] = kt.astype(BF16)
    v_ref[...] = y[:, COL_V:COL_V + KV_W].astype(BF16)
    gd_ref[...] = y[:, COL_G:COL_G + GDN_QKV_W].astype(BF16)
    ab_ref[...] = y[:, COL_AB:COL_AB + LANES]


def _inproj_call(x, shift, scale, gain, w_pack, qg, kg, cos, sin, *, latent, tm):
    b, t, d = x.shape
    nt = t // tm
    per_batch = shift.shape[0] == b
    mod_map = (lambda i, j: (i, 0, 0)) if per_batch else (lambda i, j: (0, 0, 0))
    row = lambda i, j: (i, j, 0)
    const2 = lambda i, j: (0, 0)
    in_specs = [pl.BlockSpec((None, tm, d), row),
                pl.BlockSpec((None, 1, d), mod_map),
                pl.BlockSpec((None, 1, d), mod_map),
                pl.BlockSpec((1, d), const2),
                pl.BlockSpec((d, PACK_W), const2)]
    args = [x, shift, scale, gain, w_pack]
    if latent:
        in_specs += [pl.BlockSpec((1, LANES), const2), pl.BlockSpec((1, LANES), const2),
                     pl.BlockSpec((tm, LANES), lambda i, j: (j, 0)),
                     pl.BlockSpec((tm, LANES), lambda i, j: (j, 0))]
        args += [qg, kg, cos, sin]
    else:
        in_specs += [pl.BlockSpec((1, LANES), const2)]
        args += [kg]
    kv_spec = pl.BlockSpec((None, tm, KV_W), row)
    kv_shape = jax.ShapeDtypeStruct((b, t, KV_W), BF16)
    out_specs = [kv_spec, kv_spec, pl.BlockSpec((None, tm, GDN_QKV_W), row)]
    out_shape = [kv_shape, kv_shape, jax.ShapeDtypeStruct((b, t, GDN_QKV_W), BF16)]
    if latent:
        out_specs = [pl.BlockSpec((None, tm, ATTN_W), row)] + out_specs + [pl.BlockSpec((None, tm, GDN_W), row)]
        out_shape = [jax.ShapeDtypeStruct((b, t, ATTN_W), BF16)] + out_shape + [jax.ShapeDtypeStruct((b, t, GDN_W), BF16)]
    out_specs.append(pl.BlockSpec((None, tm, LANES), row))
    out_shape.append(jax.ShapeDtypeStruct((b, t, LANES), F32))
    return pl.pallas_call(
        functools.partial(_inproj_kernel, latent=latent),
        grid=(b, nt),
        in_specs=in_specs,
        out_specs=out_specs,
        out_shape=out_shape,
        compiler_params=_cparams(("parallel", "parallel")),
        name="inproj_lat" if latent else "inproj_ctx",
    )(*args)


def _attn_kernel(q_ref, kc_ref, vc_ref, kl_ref, vl_ref, o_ref, vt_ref, *, tq, tk):
    j = pl.program_id(1)
    tc = kc_ref.shape[0]
    n_lat = kl_ref.shape[0]

    @pl.when(pl.program_id(2) == 0)
    def _():
        def put(off, v):
            vt = v.astype(F32).T
            other = (lax.broadcasted_iota(jnp.int32, vt.shape, 0) // HD + j) & 1
            vt_ref[:, off:off + v.shape[0]] = jnp.where(other == 1, 1.0, vt).astype(BF16)

        put(0, vc_ref[...])
        for c in range(n_lat // tk):
            put(tc + c * tk, vl_ref[c * tk:(c + 1) * tk, :])

    qt = q_ref[...].astype(F32).T
    parts = []
    for g in range(GQA):
        qg = qt[g * HD:(g + 1) * HD, :]
        parts.append(jnp.concatenate([jnp.where(j == 0, qg, 0.0), jnp.where(j == 1, qg, 0.0)], axis=0))
    wq = jnp.concatenate(parts, axis=1).astype(BF16)
    nq = wq.shape[1]

    segs = [(kc_ref, 0, 0, tc)] + [(kl_ref, c * tk, tc + c * tk, tk) for c in range(n_lat // tk)]

    def scores(seg):
        k_ref, start, _, size = seg
        return jnp.dot(k_ref[start:start + size, :], wq, preferred_element_type=F32)

    m =jnp.full((1, nq), -jnp.inf, F32)
    acc = jnp.zeros((KV_W, nq), F32)
    s_next = scores(segs[0])
    for idx, (_, _, voff, size) in enumerate(segs):
        s = s_next
        if idx + 1 < len(segs):
            s_next = scores(segs[idx + 1])
        m_new = jnp.maximum(m, jnp.max(s, axis=0, keepdims=True))
        alpha = jnp.exp2(m - m_new)
        p = jnp.exp2(s - m_new).astype(BF16)
        acc = alpha * acc + jnp.dot(vt_ref[:, voff:voff + size], p,
                                    preferred_element_type=F32)
        m = m_new
    ot = jnp.where(j == 0, acc[:HD], acc[HD:])
    ot = ot / jnp.where(j == 0, acc[HD:HD + 1], acc[0:1])
    og = jnp.concatenate([ot[:, g * tq:(g + 1) * tq] for g in range(GQA)], axis=0)
    o_ref[...] = og.T.astype(BF16)


def _attn_call(q, kc, vc, kl, vl, *, tq, tk):
    b, t, _ = q.shape
    tc = kc.shape[1]
    gw = GQA * HD
    kv_c = pl.BlockSpec((None, tc, KV_W), lambda i, j, n: (i, 0, 0))
    kv_l = pl.BlockSpec((None, t, KV_W), lambda i, j, n: (i, 0, 0))
    qo = pl.BlockSpec((None, tq, gw), lambda i, j, n: (i, n, j))
    return pl.pallas_call(
        functools.partial(_attn_kernel, tq=tq, tk=tk),
        grid=(b, KV_ATTN, t // tq),
        in_specs=[qo, kv_c, kv_c, kv_l, kv_l],
        out_specs=qo,
        out_shape=jax.ShapeDtypeStruct((b, t, ATTN_W), BF16),
        scratch_shapes=[pltpu.VMEM((KV_W, tc + t), BF16)],
        compiler_params=_cparams(("parallel", "arbitrary", "arbitrary")),
        name="attn",
    )(q, kc, vc, kl, vl)


def _gdn_prep_kernel(x_ref, xp_ref, xn_ref, cw_ref, ab_ref, par_ref,
                     wq_ref, ik_ref, u_ref, dl_ref, *, tb):
    c = CHUNK
    t_idx = pl.program_id(1)
    nt = pl.num_programs(1)

    x = x_ref[...].astype(F32)
    row = lax.broadcasted_iota(jnp.int32, x.shape, 0)
    prev_row = jnp.where(t_idx > 0, xp_ref[...].astype(F32)[HALO - 1:HALO, :], 0.0)
    next_row = jnp.where(t_idx < nt - 1, xn_ref[...].astype(F32)[0:1, :], 0.0)
    xm = jnp.where(row == 0, prev_row, pltpu.roll(x, 1, 0))
    xp = jnp.where(row == tb - 1, next_row, pltpu.roll(x, tb - 1, 0))
    w = cw_ref[...]

    def l2n(a):
        return a * lax.rsqrt(jnp.sum(a * a, axis=-1, keepdims=True) + EPS)

    ab = ab_ref[...]
    par = par_ref[...]
    g_all = -jnp.exp(par[0:1, :]) * jax.nn.softplus(ab + par[1:2, :])
    beta_all = jax.nn.sigmoid(ab)

    ri = lax.broadcasted_iota(jnp.int32, (c, c), 0)
    ci = lax.broadcasted_iota(jnp.int32, (c, c), 1)
    eye = (ri == ci).astype(F32)
    xr = ri ^ ci
    lane = lax.broadcasted_iota(jnp.int32, (c, LANES), 1)
    tri_f = (ri >= ci).astype(F32)
    tri_b = (ri <= ci).astype(F32)

    def blockdiag(xy):
        z = jnp.zeros((c, c), xy.dtype)
        return jnp.concatenate([jnp.concatenate([xy[:, :c], z], axis=1),
                                jnp.concatenate([z, xy[:, c:]], axis=1)], axis=0)

    def prep_chunk(n, chains):
        sl = slice(n * c, (n + 1) * c)
        y = _silu(xm[sl] * w[0:1, :] + x[sl] * w[1:2, :] + xp[sl] * w[2:3, :])
        qs = [l2n(y[:, h * DK:(h + 1) * DK]) * (DK ** -0.5) for h in range(H_GDN)]
        ks = [l2n(y[:, GDN_QK + h * DK:GDN_QK + (h + 1) * DK]) for h in range(H_GDN)]
        vs = [y[:, 2 * GDN_QK + h * DK:2 * GDN_QK + (h + 1) * DK] for h in range(H_GDN)]
        yield
        g_c = g_all[sl]
        beta_c = beta_all[sl]
        cs_f = jnp.dot(tri_f, g_c, preferred_element_type=F32, precision=lax.Precision.HIGHEST)
        cs_b = jnp.dot(tri_b, g_c, preferred_element_type=F32, precision=lax.Precision.HIGHEST)
        gc_all = jnp.where((lane & H_GDN) == 0, cs_f, cs_b)
        gc_t = gc_all.T
        kkqk_h = []
        for h0 in range(0, H_GDN, 2):
            kq2 = jnp.concatenate(
                [jnp.concatenate([ks[h], qs[h]], axis=0) for h in (h0, h0 + 1)],
                axis=1).astype(BF16)
            k2 = blockdiag(jnp.concatenate([ks[h0], ks[h0 + 1]], axis=1).astype(BF16))
            res = lax.dot_general(kq2, k2, (((1,), (1,)), ((), ())), preferred_element_type=F32)
            kkqk_h += [res[:, :c], res[:, c:]]
        yield
        for h in range(H_GDN):
            qc, kc, vc = qs[h], ks[h], vs[h]
            kk, qk = kkqk_h[h][:c], kkqk_h[h][c:]
            k_t = kc.T
            for d in range(2):
                gi = d * H_GDN + h
                bi = 2 * H_GDN + gi
                gcol = gc_all[:, gi:gi + 1]
                bcol = beta_c[:, bi:bi + 1]
                grow = gc_t[gi:gi + 1, :]
                incl = (ri >= ci) if d == 0 else (ri <= ci)
                strict = (ri > ci) if d == 0 else (ri < ci)
                dec = jnp.exp(jnp.where(incl, gcol - grow, -jnp.inf))
                a_mat = jnp.where(strict, kk * bcol * dec, 0.0)
                egc = jnp.exp(gcol)
                rhs = jnp.concatenate([vc * bcol, kc * (bcol * egc)], axis=1).astype(BF16)
                glast = gcol[c - 1:c, :] if d == 0 else gcol[0:1, :]
                wq_ref[d, h, n, c:2 * c, :] = (qc * egc).astype(BF16)
                ik_ref[d, h, n, 0:c, :] = (qk * dec).astype(BF16)
                ik_ref[d, h, n, c:c + DK, :] = (k_t * jnp.exp(glast - grow)).astype(BF16)
                dl_ref[d, h, n] = jnp.broadcast_to(jnp.exp(glast), (SUBLANES, LANES))
                chains.append((d, h, n, a_mat, rhs))
            yield

    xr2 = jnp.concatenate([xr, xr], axis=1)
    eye2 = jnp.concatenate([eye, eye], axis=1)

    def solve(chains):
        a_pairs = [jnp.concatenate([chains[i][3], chains[i + 1][3]], axis=1)
                   for i in range(0, len(chains), 2)]
        inv_pairs = [eye2 - jnp.where((xr2 >> 1) == 0, ap, 0.0) for ap in a_pairs]
        for lvl in range(1, int(np.log2(c))):
            sel = ((xr2 >> (lvl + 1)) == 0) & ((xr2 >> lvl) != 0)
            invb = [ip.astype(BF16) for ip in inv_pairs]
            mts = [jnp.dot(jnp.where(sel, ap, 0.0).astype(BF16), blockdiag(ib),
                           preferred_element_type=F32) for ap, ib in zip(a_pairs, invb)]
            yield
            inv_pairs = [ip - jnp.dot(ib, blockdiag(mt.astype(BF16)), preferred_element_type=F32)
                         for ip, ib, mt in zip(inv_pairs, invb, mts)]
            yield
        invs = [ip[:, s * c:(s + 1) * c] for ip in inv_pairs for s in range(2)]
        for (d, h, n, _, rhs), inv in zip(chains, invs):
            uw = jnp.dot(inv.astype(BF16), rhs, preferred_element_type=F32)
            u_ref[d, h, n * c:(n + 1) * c, :] = uw[:, :DK].astype(BF16)
            wq_ref[d, h, n, 0:c, :] = uw[:, DK:].astype(BF16)

    def prep_group(chunk_ids, chains):
        for n in chunk_ids:
            yield from prep_chunk(n, chains)

    n_chunks = tb // c
    group = min(GDN_GROUP, n_chunks)
    pending = []
    for _ in prep_group(range(group), pending):
        pass
    for g0 in range(group, n_chunks + group, group):
        current, pending = pending, []
        nxt = prep_group(range(g0, min(g0 + group, n_chunks)), pending)
        for _ in solve(current):
            next(nxt, None)
        for _ in nxt:
            pass


def _gdn_prep_call(gd, conv_w, ab, par, *, tb):
    b, t, _ = gd.shape
    c = CHUNK
    nchunk = t // c
    nb8 = t // HALO
    tb8 = tb // HALO
    ncb = tb // c
    in_specs = [pl.BlockSpec((None, tb, GDN_QKV_W), lambda i, j: (i, j, 0)),
                pl.BlockSpec((None, HALO, GDN_QKV_W),
                             lambda i, j: (i, jnp.maximum(j * tb8 - 1, 0), 0)),
                pl.BlockSpec((None, HALO, GDN_QKV_W),
                             lambda i, j: (i, jnp.minimum((j + 1) * tb8, nb8 - 1), 0)),
                pl.BlockSpec((3, GDN_QKV_W), lambda i, j: (0, 0)),
                pl.BlockSpec((None, tb, LANES), lambda i, j: (i, j, 0)),
                pl.BlockSpec((SUBLANES, LANES), lambda i, j: (0, 0))]
    out_specs = [
        pl.BlockSpec((None, 2, H_GDN, ncb, 2 * c, DK), lambda i, j: (i, 0, 0, j, 0, 0)),
        pl.BlockSpec((None, 2, H_GDN, ncb, c + DK, c), lambda i, j: (i, 0, 0, j, 0, 0)),
        pl.BlockSpec((None, 2, H_GDN, tb, DK), lambda i, j: (i, 0, 0, j, 0)),
        pl.BlockSpec((None, 2, H_GDN, ncb, SUBLANES, LANES), lambda i, j: (i, 0, 0, j, 0, 0)),
    ]
    out_shape = [
        jax.ShapeDtypeStruct((b, 2, H_GDN, nchunk, 2 * c, DK), BF16),
        jax.ShapeDtypeStruct((b, 2, H_GDN, nchunk, c + DK, c), BF16),
        jax.ShapeDtypeStruct((b, 2, H_GDN, t, DK), BF16),
        jax.ShapeDtypeStruct((b, 2, H_GDN, nchunk, SUBLANES, LANES), F32),
    ]
    return pl.pallas_call(
        functools.partial(_gdn_prep_kernel, tb=tb),
        grid=(b, t // tb),
        in_specs=in_specs,
        out_specs=out_specs,
        out_shape=out_shape,
        compiler_params=_cparams(("parallel", "arbitrary")),
        name="gdn_prep",
    )(gd, gd, gd, conv_w, ab, par)


def _gdn_scan_kernel(wqf_ref, ikf_ref, uf_ref, dlf_ref, wqb_ref, ikb_ref, ub_ref, dlb_ref, s0_ref,
                     *out_refs, nb, with_out):
    c = CHUNK
    if with_out:
        of_ref, ob_ref, sfin_ref, s_ref = out_refs
    else:
        sfin_ref, s_ref = out_refs
    step = pl.program_id(1)

    @pl.when(step == 0)
    def _():
        s_ref[...] = s0_ref[...]

    refs = ((wqf_ref, ikf_ref, uf_ref, dlf_ref), (wqb_ref, ikb_ref, ub_ref, dlb_ref))
    chains = [(d, h) for d in range(2) for h in range(H_GDN)]
    for n in range(nb):
        cns = [n if d == 0 else nb - 1 - n for d, _ in chains]
        ss = [s_ref[d, h] for d, h in chains]
        r1s = [jnp.dot(refs[d][0][h, cn], s.astype(BF16), preferred_element_type=F32)
               for (d, h), cn, s in zip(chains, cns, ss)]
        vns = [refs[d][2][h, cn * c:(cn + 1) * c, :].astype(F32) - r1[:c]
               for (d, h), cn, r1 in zip(chains, cns, r1s)]
        r2s = [jnp.dot(refs[d][1][h, cn], vn.astype(BF16), preferred_element_type=F32)
               for (d, h), cn, vn in zip(chains, cns, vns)]
        for (d, h), cn, s, r1, r2 in zip(chains, cns, ss, r1s, r2s):
            if with_out:
                o_ref = of_ref if d == 0 else ob_ref
                o_ref[cn * c:(cn + 1) * c, h * DK:(h + 1) * DK] = (r1[c:] + r2[:c]).astype(BF16)
            s_ref[d, h] = s * refs[d][3][h, cn, 0:1, 0:1] + r2[c:]

    @pl.when(step == pl.num_programs(1) - 1)
    def _():
        sfin_ref[...] = s_ref[...]


def _gdn_scan_call(wq, ik, u, dl, s0, *, nb, with_out):
    b = wq.shape[0]
    c = CHUNK
    nchunk = wq.shape[3]
    t = nchunk * c
    ns = nchunk // nb

    def spec(shape_tail, d, tok_scale):
        nd = len(shape_tail)
        if d == 0:
            return pl.BlockSpec((None, None, H_GDN, tok_scale) + shape_tail,
                                lambda i, j: (i, 0, 0, j) + (0,) * nd)
        return pl.BlockSpec((None, None, H_GDN, tok_scale) + shape_tail,
                            lambda i, j: (i, 1, 0, ns - 1 - j) + (0,) * nd)

    in_specs = []
    for d in range(2):
        in_specs += [spec((2 * c, DK), d, nb), spec((c + DK, c), d, nb),
                     spec((DK,), d, nb * c), spec((SUBLANES, LANES), d, nb)]
    in_specs.append(pl.BlockSpec((None, 2, H_GDN, DK, DK), lambda i, j: (i, 0, 0, 0, 0)))
    out_specs = []
    out_shape = []
    if with_out:
        out_specs += [pl.BlockSpec((None, nb * c, GDN_W), lambda i, j: (i, j, 0)),
                      pl.BlockSpec((None, nb * c, GDN_W), lambda i, j: (i, ns - 1 - j, 0))]
        out_shape += [jax.ShapeDtypeStruct((b, t, GDN_W), BF16)] * 2
    out_specs.append(pl.BlockSpec((None, 2, H_GDN, DK, DK), lambda i, j: (i, 0, 0, 0, 0)))
    out_shape.append(jax.ShapeDtypeStruct((b, 2, H_GDN, DK, DK), F32))
    return pl.pallas_call(
        functools.partial(_gdn_scan_kernel, nb=nb, with_out=with_out),
        grid=(b, ns),
        in_specs=in_specs,
        out_specs=out_specs,
        out_shape=out_shape,
        scratch_shapes=[pltpu.VMEM((2, H_GDN, DK, DK), F32)],
        compiler_params=_cparams(("parallel", "arbitrary")),
        name="gdn_scan" if with_out else "gdn_scan_ctx",
    )(wq, ik, u, dl, wq, ik, u, dl, s0)


def _mix_ffn_kernel(x_ref, xp_ref, xn_ref, oa_ref, oap_ref, oan_ref, of_ref, ofp_ref, ofn_ref,
                    ob_ref, obp_ref, obn_ref, z_ref, zp_ref, zn_ref,
                    g1_ref, ag_ref, gg_ref, wo_ref,
                    sh_ref, sc_ref, g2_ref, ng_ref, wu_ref, cw_ref, cb_ref, wd_ref, fg_ref,
                    o_ref, *, tm, nc):
    t_idx = pl.program_id(1)
    nt = pl.num_programs(1)
    d_ff = wd_ref.shape[0]

    def rows3(a, ap, an):
        return jnp.concatenate([a[...], ap[...], an[...]], axis=0).astype(F32)

    oa = _rms(rows3(oa_ref, oap_ref, oan_ref), ag_ref[...])
    og = rows3(of_ref, ofp_ref, ofn_ref) + rows3(ob_ref, obp_ref, obn_ref)
    z = rows3(z_ref, zp_ref, zn_ref)
    gg = gg_ref[...]
    parts = [oa.astype(BF16)]
    for hd in range(H_GDN):
        sl = slice(hd * DK, (hd + 1) * DK)
        parts.append((_rms(og[:, sl], gg) * _silu(z[:, sl])).astype(BF16))
    lat = jnp.dot(jnp.concatenate(parts, axis=1), wo_ref[...], preferred_element_type=F32)
    xa = rows3(x_ref, xp_ref, xn_ref) + g1_ref[...] * lat
    x = xa[:tm]

    ng, sc, sh = ng_ref[...], sc_ref[...], sh_ref[...]
    h = (_rms(xa, ng) * (1.0 + sc) + sh).astype(BF16)
    row8 = lax.broadcasted_iota(jnp.int32, (SUBLANES, nc), 0)
    has_prev = t_idx > 0
    has_next = t_idx < nt - 1

    def up_proj(c):
        return [jnp.dot(h, wu_ref[:, base:base + nc], preferred_element_type=F32)
                for base in (c * nc, d_ff + c * nc)]

    def conv_act(c, ups):
        us = []
        for base, ua in zip((c * nc, d_ff + c * nc), ups):
            up = ua[:tm]
            prev_row = jnp.where(has_prev, ua[tm + HALO - 1:tm + HALO, :], 0.0)
            next_row = jnp.where(has_next, ua[tm + HALO:tm + HALO + 1, :], 0.0)
            dn = pltpu.roll(up, 1, 0)
            um = jnp.concatenate([jnp.where(row8 == 0, prev_row, dn[:SUBLANES]), dn[SUBLANES:]], axis=0)
            nx = pltpu.roll(up, tm - 1, 0)
            upn = jnp.concatenate([nx[:tm - SUBLANES],
                                   jnp.where(row8 == SUBLANES - 1, next_row, nx[tm - SUBLANES:])], axis=0)
            w = cw_ref[:, base:base + nc]
            us.append(um * w[0:1, :] + up * w[1:2, :] + upn * w[2:3, :] + cb_ref[:, base:base + nc])
        return (_silu(us[0]) * us[1]).astype(BF16)

    n_chunks = d_ff // nc
    acts = []
    ups = up_proj(0)
    for c in range(n_chunks):
        ups_next = up_proj(c + 1) if c + 1 < n_chunks else None
        acts.append(conv_act(c, ups))
        ups = ups_next
    acc = jnp.dot(jnp.concatenate(acts, axis=1), wd_ref[...], preferred_element_type=F32)
    x2 = x + g2_ref[...] * acc
    o_ref[...] = _rms(x2, fg_ref[...])


def _mix_ffn_call(x, oa, of, ob, z, g1, ag, gg, w_out,
                  sh2, sc2, g2, ng, w_up, conv_w, conv_b, w_down, fg, *, tm, nc):
    b, t, d = x.shape
    d_ff = w_down.shape[0]
    nb8 = t // HALO
    tm8 = tm // HALO
    mod = lambda i, j: (i, 0, 0)
    const2 = lambda i, j: (0, 0)

    def with_halo(width):
        return [pl.BlockSpec((None, tm, width), lambda i, j: (i, j, 0)),
                pl.BlockSpec((None, HALO, width),
                             lambda i, j: (i, jnp.maximum(j * tm8 - 1, 0), 0)),
                pl.BlockSpec((None, HALO, width),
                             lambda i, j: (i, jnp.minimum((j + 1) * tm8, nb8 - 1), 0))]

    def resident(shape):
        return pl.BlockSpec(shape, const2, pipeline_mode=pl.Buffered(1))

    in_specs = (with_halo(d) + with_halo(ATTN_W) + with_halo(GDN_W) + with_halo(GDN_W)
                + with_halo(GDN_W)
                + [pl.BlockSpec((None, 1, d), mod),
                   pl.BlockSpec((1, ATTN_W), const2),
                   pl.BlockSpec((1, DK), const2),
                   resident((ATTN_W + GDN_W, d)),
                   pl.BlockSpec((None, 1, d), mod),
                   pl.BlockSpec((None, 1, d), mod),
                   pl.BlockSpec((None, 1, d), mod),
                   pl.BlockSpec((1, d), const2),
                   resident((d, 2 * d_ff)),
                   pl.BlockSpec((3, 2 * d_ff), const2),
                   pl.BlockSpec((1, 2 * d_ff), const2),
                   resident((d_ff, d)),
                   pl.BlockSpec((1, d), const2)])
    return pl.pallas_call(
        functools.partial(_mix_ffn_kernel, tm=tm, nc=nc),
        grid=(b, t // tm),
        in_specs=in_specs,
        out_specs=pl.BlockSpec((None, tm, d), lambda i, j: (i, j, 0)),
        out_shape=jax.ShapeDtypeStruct((b, t, d), F32),
        compiler_params=_cparams(("parallel", "parallel")),
        name="mix_ffn",
    )(x, x, x, oa, oa, oa, of, of, of, ob, ob, ob, z, z, z, g1, ag, gg, w_out,
      sh2, sc2, g2, ng, w_up, conv_w, conv_b, w_down, fg)


def _rope_tables(t):
    rows = t // GRID_W
    r = jnp.repeat(jnp.arange(rows, dtype=F32), GRID_W)
    c = jnp.tile(jnp.arange(GRID_W, dtype=F32), rows)
    pairs = HD // 4
    inv_freq = ROPE_THETA ** (-jnp.arange(pairs, dtype=F32) / pairs)
    ang = jnp.concatenate([r[:, None] * inv_freq, c[:, None] * inv_freq], axis=-1)
    cos, sin = jnp.cos(ang), jnp.sin(ang)
    cos128 = jnp.tile(cos, (1, LANES // (HD // 2)))
    sin128 = jnp.tile(jnp.concatenate([-sin, sin], axis=-1), (1, LANES // HD))
    return cos128, sin128


def _pick(n, candidates):
    for c in candidates:
        if n % c == 0:
            return c
    raise ValueError(f"no tile size in {candidates} divides {n}")


def kernel(x, c, ctx, c_ctx, w_mod, b_mod, norm1_g, w_in, q_norm_g, k_norm_g, attn_out_g,
           conv_qkv_w, a_log_f, a_log_b, dt_bias_f, dt_bias_b, gdn_norm_g, w_out, norm2_g,
           w_up, ffn_conv_w, ffn_conv_b, w_down, final_norm_g):
    depth = w_mod.shape[0]
    assert depth == 1, "only the single-layer configuration is implemented"
    b, t, d = x.shape
    tc = ctx.shape[1]
    assert t % CHUNK == 0 and tc % CHUNK == 0 and t % GRID_W == 0

    rows = -(-(b + 1) // SUBLANES) * SUBLANES
    cc = jnp.zeros((rows, d), F32).at[:b].set(c).at[b].set(c_ctx)
    mod = _mod_call(cc, w_mod[0], b_mod[0]).reshape(rows, N_MOD, d)
    lat = [mod[:b, i][:, None, :] for i in range(N_MOD)]
    cm = [mod[b:b + 1, i][:, None, :] for i in range(2)]

    w_pack = jnp.pad(w_in[0], ((0, 0), (0, PACK_W - w_in.shape[-1]))).astype(BF16)
    qg = jnp.tile(q_norm_g[0], LANES // HD)[None, :]
    kg = jnp.tile(k_norm_g[0], LANES // HD)[None, :]
    cos, sin = _rope_tables(t)
    g1n = norm1_g[0][None, :]

    tm = _pick(t, (512, 256, 128))
    q, kl, vl, gdl, z, abl = _inproj_call(x, lat[0], lat[1], g1n, w_pack, qg, kg, cos, sin,
                                          latent=True, tm=tm)
    kc, vc, gdc, abc = _inproj_call(ctx, cm[0], cm[1], g1n, w_pack, None, kg, None, None,
                                    latent=False, tm=_pick(tc, (256, 128)))

    o_attn = _attn_call(q, kc, vc, kl, vl, tq=_pick(t, (1024, 512, 256, 128)), tk=_pick(t, (512, 256, 128)))

    par = jnp.zeros((SUBLANES, LANES), F32)
    par = par.at[0, :2 * H_GDN].set(jnp.concatenate([a_log_f[0], a_log_b[0]]))
    par = par.at[1, :2 * H_GDN].set(jnp.concatenate([dt_bias_f[0], dt_bias_b[0]]))
    cw = conv_qkv_w[0]
    prep_c = _gdn_prep_call(gdc, cw, abc, par, tb=_pick(tc, (256, 128)))
    s0 = jnp.zeros((b, 2, H_GDN, DK, DK), F32)
    (s_ctx,) = _gdn_scan_call(*prep_c, s0, nb=tc // CHUNK, with_out=False)
    prep_l = _gdn_prep_call(gdl, cw, abl, par, tb=_pick(t, (512, 256, 128)))
    o_f, o_b, _ = _gdn_scan_call(*prep_l, s_ctx, nb=_pick(t // CHUNK, (4, 2, 1)), with_out=True)

    return _mix_ffn_call(x, o_attn, o_f, o_b, z, lat[2], attn_out_g[0][None, :],
                         gdn_norm_g[0][None, :], w_out[0].astype(BF16),
                         lat[3], lat[4], lat[5], norm2_g[0][None, :], w_up[0].astype(BF16),
                         ffn_conv_w[0], ffn_conv_b[0][None, :], w_down[0].astype(BF16),
                         final_norm_g[None, :], tm=tm, nc=256)
```

```python
import functools

import jax
import jax.numpy as jnp
import numpy as np
from jax import lax
from jax.experimental import pallas as pl
from jax.experimental.pallas import tpu as pltpu

F32 = jnp.float32
BF16 = jnp.bfloat16

GRID_W = 64
H_ATTN = 8
KV_ATTN = 2
HD = 64
GQA = H_ATTN // KV_ATTN
ATTN_W = H_ATTN * HD
KV_W = KV_ATTN * HD
H_GDN = 4
DK = 128
GDN_QK = H_GDN * DK
GDN_QKV_W = 3 * GDN_QK
GDN_W = H_GDN * DK
N_MOD = 6
ROPE_THETA = 10000.0
EPS = 1e-6
Q_SCALE = HD ** -0.5 * float(np.log2(np.e))

LANES = 128
SUBLANES = 8
HALO = 16
VMEM_LIMIT_BYTES = 56 * 1024 * 1024
GDN_VMEM_LIMIT_BYTES = 62 * 1024 * 1024

CHUNK = 128
GDN_GROUP = 2

COL_Q = 0
COL_K = COL_Q + ATTN_W
COL_V = COL_K + KV_W
COL_G = COL_V + KV_W
COL_Z = COL_G + GDN_QKV_W
COL_AB = COL_Z + GDN_W
PACK_W = COL_AB + LANES


def _cparams(sem, vmem_limit_bytes=VMEM_LIMIT_BYTES):
    return pltpu.CompilerParams(dimension_semantics=sem, vmem_limit_bytes=vmem_limit_bytes)


def _silu(x):
    return x * jax.nn.sigmoid(x)


def _rms(x, gain):
    return x * lax.rsqrt(jnp.mean(x * x, axis=-1, keepdims=True) + EPS) * gain


def _mod_kernel(c_ref, w_ref, b_ref, o_ref):
    s = _silu(c_ref[...])
    o_ref[...] = jnp.dot(s, w_ref[...], preferred_element_type=F32,
                         precision=lax.Precision.HIGHEST) + b_ref[...]


def _mod_call(cc, w_mod, b_mod):
    rows, d = cc.shape
    n = w_mod.shape[1]
    bn = 1024
    return pl.pallas_call(
        _mod_kernel,
        grid=(n // bn,),
        in_specs=[pl.BlockSpec((rows, d), lambda j: (0, 0)),
                  pl.BlockSpec((d, bn), lambda j: (0, j)),
                  pl.BlockSpec((1, bn), lambda j: (0, j))],
        out_specs=pl.BlockSpec((rows, bn), lambda j: (0, j)),
        out_shape=jax.ShapeDtypeStruct((rows, n), F32),
        compiler_params=_cparams(("arbitrary",)),
        name="mod",
    )(cc, w_mod, b_mod.reshape(1, n))


def _low_half(shape):
    return (lax.broadcasted_iota(jnp.int32, shape, 1) & HD) == 0


def _head_norm(t, gain):
    lo = _low_half(t.shape)
    sq = t * t
    s_lo = jnp.sum(jnp.where(lo, sq, 0.0), axis=-1, keepdims=True)
    s_hi = jnp.sum(jnp.where(lo, 0.0, sq), axis=-1, keepdims=True)
    ms = jnp.where(lo, s_lo, s_hi) * (1.0 / HD)
    return t * lax.rsqrt(ms + EPS) * gain


def _rope(t, cos, sin_signed):
    first = (lax.broadcasted_iota(jnp.int32, t.shape, 1) & (HD // 2)) == 0
    partner = jnp.where(first, pltpu.roll(t, LANES - HD // 2, 1), pltpu.roll(t, HD // 2, 1))
    return t * cos + partner * sin_signed


def _inproj_kernel(*refs, latent):
    if latent:
        (x_ref, sh_ref, sc_ref, g_ref, w_ref, qg_ref, kg_ref, cos_ref, sin_ref,
         q_ref, k_ref, v_ref, gd_ref, z_ref, ab_ref) = refs
    else:
        (x_ref, sh_ref, sc_ref, g_ref, w_ref, kg_ref,
         k_ref, v_ref, gd_ref, ab_ref) = refs
    x = x_ref[...]
    h = _rms(x, g_ref[...]) * (1.0 + sc_ref[...]) + sh_ref[...]
    y = jnp.dot(h.astype(BF16), w_ref[...], preferred_element_type=F32)

    kt = _head_norm(y[:, COL_K:COL_K + KV_W], kg_ref[...])
    if latent:
        cos = cos_ref[...]
        sin = sin_ref[...]
        kt = _rope(kt, cos, sin)
        for i in range(ATTN_W // LANES):
            qt = y[:, COL_Q + i * LANES:COL_Q + (i + 1) * LANES]
            qt = _rope(_head_norm(qt, qg_ref[...]), cos, sin) * Q_SCALE
            q_ref[:, i * LANES:(i + 1) * LANES] = qt.astype(BF16)
        z_ref[...] = y[:, COL_Z:COL_Z + GDN_W].astype(BF16)
    k_ref[...] = kt.astype(BF16)
    v_ref[...] = y[:, COL_V:COL_V + KV_W].astype(BF16)
    gd_ref[...] = y[:, COL_G:COL_G + GDN_QKV_W].astype(BF16)
    ab_ref[...] = y[:, COL_AB:COL_AB + LANES]


def _inproj_call(x, shift, scale, gain, w_pack, qg, kg, cos, sin, *, latent, tm):
    b, t, d = x.shape
    nt = t // tm
    per_batch = shift.shape[0] == b
    mod_map = (lambda i, j: (i, 0, 0)) if per_batch else (lambda i, j: (0, 0, 0))
    row = lambda i, j: (i, j, 0)
    const2 = lambda i, j: (0, 0)
    in_specs = [pl.BlockSpec((None, tm, d), row),
                pl.BlockSpec((None, 1, d), mod_map),
                pl.BlockSpec((None, 1, d), mod_map),
                pl.BlockSpec((1, d), const2),
                pl.BlockSpec((d, PACK_W), const2)]
    args = [x, shift, scale, gain, w_pack]
    if latent:
        in_specs += [pl.BlockSpec((1, LANES), const2), pl.BlockSpec((1, LANES), const2),
                     pl.BlockSpec((tm, LANES), lambda i, j: (j, 0)),
                     pl.BlockSpec((tm, LANES), lambda i, j: (j, 0))]
        args += [qg, kg, cos, sin]
    else:
        in_specs += [pl.BlockSpec((1, LANES), const2)]
        args += [kg]
    kv_spec = pl.BlockSpec((None, tm, KV_W), row)
    kv_shape = jax.ShapeDtypeStruct((b, t, KV_W), BF16)
    out_specs = [kv_spec, kv_spec, pl.BlockSpec((None, tm, GDN_QKV_W), row)]
    out_shape = [kv_shape, kv_shape, jax.ShapeDtypeStruct((b, t, GDN_QKV_W), BF16)]
    if latent:
        out_specs = [pl.BlockSpec((None, tm, ATTN_W), row)] + out_specs + [pl.BlockSpec((None, tm, GDN_W), row)]
        out_shape = [jax.ShapeDtypeStruct((b, t, ATTN_W), BF16)] + out_shape + [jax.ShapeDtypeStruct((b, t, GDN_W), BF16)]
    out_specs.append(pl.BlockSpec((None, tm, LANES), row))
    out_shape.append(jax.ShapeDtypeStruct((b, t, LANES), F32))
    return pl.pallas_call(
        functools.partial(_inproj_kernel, latent=latent),
        grid=(b, nt),
        in_specs=in_specs,
        out_specs=out_specs,
        out_shape=out_shape,
        compiler_params=_cparams(("parallel", "parallel")),
        name="inproj_lat" if latent else "inproj_ctx",
    )(*args)


def _attn_kernel(q_ref, kc_ref, vc_ref, kl_ref, vl_ref, o_ref, vt_ref, *, tq, tk):
    j = pl.program_id(1)
    tc = kc_ref.shape[0]
    n_lat = kl_ref.shape[0]

    @pl.when(pl.program_id(2) == 0)
    def _():
        def put(off, v):
            vt = v.astype(F32).T
            other = (lax.broadcasted_iota(jnp.int32, vt.shape, 0) // HD + j) & 1
            vt_ref[:, off:off + v.shape[0]] = jnp.where(other == 1, 1.0, vt).astype(BF16)

        put(0, vc_ref[...])
        for c in range(n_lat // tk):
            put(tc + c * tk, vl_ref[c * tk:(c + 1) * tk, :])

    qt = q_ref[...].astype(F32).T
    parts = []
    for g in range(GQA):
        qg = qt[g * HD:(g + 1) * HD, :]
        parts.append(jnp.concatenate([jnp.where(j == 0, qg, 0.0), jnp.where(j == 1, qg, 0.0)], axis=0))
    wq = jnp.concatenate(parts, axis=1).astype(BF16)
    nq = wq.shape[1]

    segs = [(kc_ref, 0, 0, tc)] + [(kl_ref, c * tk, tc + c * tk, tk) for c in range(n_lat // tk)]

    def scores(seg):
        k_ref, start, _, size = seg
        return jnp.dot(k_ref[start:start + size, :], wq, preferred_element_type=F32)

    m =jnp.full((1, nq), -jnp.inf, F32)
    acc = jnp.zeros((KV_W, nq), F32)
    s_next = scores(segs[0])
    for idx, (_, _, voff, size) in enumerate(segs):
        s = s_next
        if idx + 1 < len(segs):
            s_next = scores(segs[idx + 1])
        m_new = jnp.maximum(m, jnp.max(s, axis=0, keepdims=True))
        alpha = jnp.exp2(m - m_new)
        p = jnp.exp2(s - m_new).astype(BF16)
        acc = alpha * acc + jnp.dot(vt_ref[:, voff:voff + size], p,
                                    preferred_element_type=F32)
        m = m_new
    ot = jnp.where(j == 0, acc[:HD], acc[HD:])
    ot = ot / jnp.where(j == 0, acc[HD:HD + 1], acc[0:1])
    og = jnp.concatenate([ot[:, g * tq:(g + 1) * tq] for g in range(GQA)], axis=0)
    o_ref[...] = og.T.astype(BF16)


def _attn_call(q, kc, vc, kl, vl, *, tq, tk):
    b, t, _ = q.shape
    tc = kc.shape[1]
    gw = GQA * HD
    kv_c = pl.BlockSpec((None, tc, KV_W), lambda i, j, n: (i, 0, 0))
    kv_l = pl.BlockSpec((None, t, KV_W), lambda i, j, n: (i, 0, 0))
    qo = pl.BlockSpec((None, tq, gw), lambda i, j, n: (i, n, j))
    return pl.pallas_call(
        functools.partial(_attn_kernel, tq=tq, tk=tk),
        grid=(b, KV_ATTN, t // tq),
        in_specs=[qo, kv_c, kv_c, kv_l, kv_l],
        out_specs=qo,
        out_shape=jax.ShapeDtypeStruct((b, t, ATTN_W), BF16),
        scratch_shapes=[pltpu.VMEM((KV_W, tc + t), BF16)],
        compiler_params=_cparams(("parallel", "arbitrary", "arbitrary")),
        name="attn",
    )(q, kc, vc, kl, vl)


def _gdn_prep(x_ref, xp_ref, xn_ref, cw_ref, ab_ref, par_ref,
              wq_ref, ik_ref, u_ref, dl_ref, *, tb, t_idx, nt):
    c = CHUNK
    chunk0 = t_idx * (tb // c)

    def rows_of(n):
        return pl.ds(pl.multiple_of((chunk0 + n) * c, c), c)

    x = x_ref[...].astype(F32)
    row = lax.broadcasted_iota(jnp.int32, x.shape, 0)
    prev_row = jnp.where(t_idx > 0, xp_ref[...].astype(F32)[HALO - 1:HALO, :], 0.0)
    next_row = jnp.where(t_idx < nt - 1, xn_ref[...].astype(F32)[0:1, :], 0.0)
    xm = jnp.where(row == 0, prev_row, pltpu.roll(x, 1, 0))
    xp = jnp.where(row == tb - 1, next_row, pltpu.roll(x, tb - 1, 0))
    w = cw_ref[...]

    def l2n(a):
        return a * lax.rsqrt(jnp.sum(a * a, axis=-1, keepdims=True) + EPS)

    ab = ab_ref[...]
    par = par_ref[...]
    g_all = -jnp.exp(par[0:1, :]) * jax.nn.softplus(ab + par[1:2, :])
    beta_all = jax.nn.sigmoid(ab)

    ri = lax.broadcasted_iota(jnp.int32, (c, c), 0)
    ci = lax.broadcasted_iota(jnp.int32, (c, c), 1)
    eye = (ri == ci).astype(F32)
    xr = ri ^ ci
    lane = lax.broadcasted_iota(jnp.int32, (c, LANES), 1)
    tri_f = (ri >= ci).astype(F32)
    tri_b = (ri <= ci).astype(F32)

    def blockdiag(xy):
        z = jnp.zeros((c, c), xy.dtype)
        return jnp.concatenate([jnp.concatenate([xy[:, :c], z], axis=1),
                                jnp.concatenate([z, xy[:, c:]], axis=1)], axis=0)

    def prep_chunk(n, chains):
        sl = slice(n * c, (n + 1) * c)
        y = _silu(xm[sl] * w[0:1, :] + x[sl] * w[1:2, :] + xp[sl] * w[2:3, :])
        qs = [l2n(y[:, h * DK:(h + 1) * DK]) * (DK ** -0.5) for h in range(H_GDN)]
        ks = [l2n(y[:, GDN_QK + h * DK:GDN_QK + (h + 1) * DK]) for h in range(H_GDN)]
        vs = [y[:, 2 * GDN_QK + h * DK:2 * GDN_QK + (h + 1) * DK] for h in range(H_GDN)]
        yield
        g_c = g_all[sl]
        beta_c = beta_all[sl]
        cs_f = jnp.dot(tri_f, g_c, preferred_element_type=F32, precision=lax.Precision.HIGHEST)
        cs_b = jnp.dot(tri_b, g_c, preferred_element_type=F32, precision=lax.Precision.HIGHEST)
        gc_all = jnp.where((lane & H_GDN) == 0, cs_f, cs_b)
        gc_t = gc_all.T
        kkqk_h = []
        for h0 in range(0, H_GDN, 2):
            kq2 = jnp.concatenate(
                [jnp.concatenate([ks[h], qs[h]], axis=0) for h in (h0, h0 + 1)],
                axis=1).astype(BF16)
            k2 = blockdiag(jnp.concatenate([ks[h0], ks[h0 + 1]], axis=1).astype(BF16))
            res = lax.dot_general(kq2, k2, (((1,), (1,)), ((), ())), preferred_element_type=F32)
            kkqk_h += [res[:, :c], res[:, c:]]
        yield
        for h in range(H_GDN):
            qc, kc, vc = qs[h], ks[h], vs[h]
            kk, qk = kkqk_h[h][:c], kkqk_h[h][c:]
            k_t = kc.T
            for d in range(2):
                gi = d * H_GDN + h
                bi = 2 * H_GDN + gi
                gcol = gc_all[:, gi:gi + 1]
                bcol = beta_c[:, bi:bi + 1]
                grow = gc_t[gi:gi + 1, :]
                incl = (ri >= ci) if d == 0 else (ri <= ci)
                strict = (ri > ci) if d == 0 else (ri < ci)
                dec = jnp.exp(jnp.where(incl, gcol - grow, -jnp.inf))
                a_mat = jnp.where(strict, kk * bcol * dec, 0.0)
                egc = jnp.exp(gcol)
                rhs = jnp.concatenate([vc * bcol, kc * (bcol * egc)], axis=1).astype(BF16)
                glast = gcol[c - 1:c, :] if d == 0 else gcol[0:1, :]
                wq_ref[d, h, chunk0 + n, c:2 * c, :] = (qc * egc).astype(BF16)
                ik_ref[d, h, chunk0 + n, 0:c, :] = (qk * dec).astype(BF16)
                ik_ref[d, h, chunk0 + n, c:c + DK, :] = (k_t * jnp.exp(glast - grow)).astype(BF16)
                dl_ref[d, h, chunk0 + n] = jnp.broadcast_to(jnp.exp(glast), (SUBLANES, LANES))
                chains.append((d, h, n, a_mat, rhs))
            yield

    xr2 = jnp.concatenate([xr, xr], axis=1)
    eye2 = jnp.concatenate([eye, eye], axis=1)

    def solve(chains):
        a_pairs = [jnp.concatenate([chains[i][3], chains[i + 1][3]], axis=1)
                   for i in range(0, len(chains), 2)]
        inv_pairs = [eye2 - jnp.where((xr2 >> 1) == 0, ap, 0.0) for ap in a_pairs]
        for lvl in range(1, int(np.log2(c))):
            sel = ((xr2 >> (lvl + 1)) == 0) & ((xr2 >> lvl) != 0)
            invb = [ip.astype(BF16) for ip in inv_pairs]
            mts = [jnp.dot(jnp.where(sel, ap, 0.0).astype(BF16), blockdiag(ib),
                           preferred_element_type=F32) for ap, ib in zip(a_pairs, invb)]
            yield
            inv_pairs = [ip - jnp.dot(ib, blockdiag(mt.astype(BF16)), preferred_element_type=F32)
                         for ip, ib, mt in zip(inv_pairs, invb, mts)]
            yield
        invs = [ip[:, s * c:(s + 1) * c] for ip in inv_pairs for s in range(2)]
        for (d, h, n, _, rhs), inv in zip(chains, invs):
            uw = jnp.dot(inv.astype(BF16), rhs, preferred_element_type=F32)
            u_ref[d, h, rows_of(n), :] = uw[:, :DK].astype(BF16)
            wq_ref[d, h, chunk0 + n, 0:c, :] = uw[:, DK:].astype(BF16)

    def prep_group(chunk_ids, chains):
        for n in chunk_ids:
            yield from prep_chunk(n, chains)

    n_chunks = tb // c
    group = min(GDN_GROUP, n_chunks)
    pending = []
    for _ in prep_group(range(group), pending):
        pass
    for g0 in range(group, n_chunks + group, group):
        current, pending = pending, []
        nxt = prep_group(range(g0, min(g0 + group, n_chunks)), pending)
        for _ in solve(current):
            next(nxt, None)
        for _ in nxt:
            pass


def _gdn_scan(wq_ref, ik_ref, u_ref, dl_ref, s_ref, of_ref, ob_ref, *, nb, j, ns):
    c = CHUNK
    base = (j * nb, (ns - 1 - j) * nb)
    chains = [(d, h) for d in range(2) for h in range(H_GDN)]
    for n in range(nb):
        cns = [n if d == 0 else nb - 1 - n for d, _ in chains]
        ss = [s_ref[d, h] for d, h in chains]
        r1s = [jnp.dot(wq_ref[d, h, base[d] + cn], s.astype(BF16), preferred_element_type=F32)
               for (d, h), cn, s in zip(chains, cns, ss)]
        vns = [u_ref[d, h, pl.ds(pl.multiple_of((base[d] + cn) * c, c), c), :].astype(F32) - r1[:c]
               for (d, h), cn, r1 in zip(chains, cns, r1s)]
        r2s = [jnp.dot(ik_ref[d, h, base[d] + cn], vn.astype(BF16), preferred_element_type=F32)
               for (d, h), cn, vn in zip(chains, cns, vns)]
        for (d, h), cn, s, r1, r2 in zip(chains, cns, ss, r1s, r2s):
            if of_ref is not None:
                o_ref = of_ref if d == 0 else ob_ref
                o_ref[cn * c:(cn + 1) * c, h * DK:(h + 1) * DK] = (r1[c:] + r2[:c]).astype(BF16)
            s_ref[d, h] = s * dl_ref[d, h, base[d] + cn, 0:1, 0:1] + r2[c:]


def _gdn_kernel(x_ref, xp_ref, xn_ref, cw_ref, ab_ref, par_ref, s0_ref, *refs,
                tb, nb, n_prep, n_scan, with_out):
    if with_out:
        of_ref, ob_ref, sfin_ref, wq_s, ik_s, u_s, dl_s, s_ref = refs
    else:
        sfin_ref, wq_s, ik_s, u_s, dl_s, s_ref = refs
        of_ref = ob_ref = None
    step = pl.program_id(1)

    @pl.when(step < n_prep)
    def _():
        _gdn_prep(x_ref, xp_ref, xn_ref, cw_ref, ab_ref, par_ref, wq_s, ik_s, u_s, dl_s,
                  tb=tb, t_idx=step, nt=n_prep)

    @pl.when(step == n_prep)
    def _():
        s_ref[...] = s0_ref[...]

    @pl.when(step >= n_prep)
    def _():
        _gdn_scan(wq_s, ik_s, u_s, dl_s, s_ref, of_ref, ob_ref, nb=nb, j=step - n_prep, ns=n_scan)

    @pl.when(step == n_prep + n_scan - 1)
    def _():
        sfin_ref[...] = s_ref[...]


def _gdn_call(gd, conv_w, ab, par, s0, *, tb, nb, with_out):
    b, t, _ = gd.shape
    c = CHUNK
    nchunk = t // c
    n_prep = t // tb
    n_scan = nchunk // nb
    nb8 = t // HALO
    tb8 = tb // HALO
    blk = lambda j: jnp.minimum(j, n_prep - 1)
    sj = lambda j: jnp.maximum(j - n_prep, 0)
    in_specs = [pl.BlockSpec((None, tb, GDN_QKV_W), lambda i, j: (i, blk(j), 0)),
                pl.BlockSpec((None, HALO, GDN_QKV_W),
                             lambda i, j: (i, jnp.maximum(blk(j) * tb8 - 1, 0), 0)),
                pl.BlockSpec((None, HALO, GDN_QKV_W),
                             lambda i, j: (i, jnp.minimum((blk(j) + 1) * tb8, nb8 - 1), 0)),
                pl.BlockSpec((3, GDN_QKV_W), lambda i, j: (0, 0)),
                pl.BlockSpec((None, tb, LANES), lambda i, j: (i, blk(j), 0)),
                pl.BlockSpec((SUBLANES, LANES), lambda i, j: (0, 0)),
                pl.BlockSpec((None, 2, H_GDN, DK, DK), lambda i, j: (i, 0, 0, 0, 0))]
    out_specs = []
    out_shape = []
    if with_out:
        out_specs += [pl.BlockSpec((None, nb * c, GDN_W), lambda i, j: (i, sj(j), 0)),
                      pl.BlockSpec((None, nb * c, GDN_W), lambda i, j: (i, n_scan - 1 - sj(j), 0))]
        out_shape += [jax.ShapeDtypeStruct((b, t, GDN_W), BF16)] * 2
    out_specs.append(pl.BlockSpec((None, 2, H_GDN, DK, DK), lambda i, j: (i, 0, 0, 0, 0)))
    out_shape.append(jax.ShapeDtypeStruct((b, 2, H_GDN, DK, DK), F32))
    return pl.pallas_call(
        functools.partial(_gdn_kernel, tb=tb, nb=nb, n_prep=n_prep, n_scan=n_scan,
                          with_out=with_out),
        grid=(b, n_prep + n_scan),
        in_specs=in_specs,
        out_specs=out_specs,
        out_shape=out_shape,
        scratch_shapes=[pltpu.VMEM((2, H_GDN, nchunk, 2 * c, DK), BF16),
                        pltpu.VMEM((2, H_GDN, nchunk, c + DK, c), BF16),
                        pltpu.VMEM((2, H_GDN, t, DK), BF16),
                        pltpu.VMEM((2, H_GDN, nchunk, SUBLANES, LANES), F32),
                        pltpu.VMEM((2, H_GDN, DK, DK), F32)],
        compiler_params=_cparams(("parallel", "arbitrary"), GDN_VMEM_LIMIT_BYTES),
        name="gdn" if with_out else "gdn_ctx",
    )(gd, gd, gd, conv_w, ab, par, s0)


def _mix_ffn_kernel(x_ref, xp_ref, xn_ref, oa_ref, oap_ref, oan_ref, of_ref, ofp_ref, ofn_ref,
                    ob_ref, obp_ref, obn_ref, z_ref, zp_ref, zn_ref,
                    g1_ref, ag_ref, gg_ref, wo_ref,
                    sh_ref, sc_ref, g2_ref, ng_ref, wu_ref, cw_ref, cb_ref, wd_ref, fg_ref,
                    o_ref, *, tm, nc):
    t_idx = pl.program_id(1)
    nt = pl.num_programs(1)
    d_ff = wd_ref.shape[0]

    def rows3(a, ap, an):
        return jnp.concatenate([a[...], ap[...], an[...]], axis=0).astype(F32)

    oa = _rms(rows3(oa_ref, oap_ref, oan_ref), ag_ref[...])
    og = rows3(of_ref, ofp_ref, ofn_ref) + rows3(ob_ref, obp_ref, obn_ref)
    z = rows3(z_ref, zp_ref, zn_ref)
    gg = gg_ref[...]
    parts = [oa.astype(BF16)]
    for hd in range(H_GDN):
        sl = slice(hd * DK, (hd + 1) * DK)
        parts.append((_rms(og[:, sl], gg) * _silu(z[:, sl])).astype(BF16))
    lat = jnp.dot(jnp.concatenate(parts, axis=1), wo_ref[...], preferred_element_type=F32)
    xa = rows3(x_ref, xp_ref, xn_ref) + g1_ref[...] * lat
    x = xa[:tm]

    ng, sc, sh = ng_ref[...], sc_ref[...], sh_ref[...]
    h = (_rms(xa, ng) * (1.0 + sc) + sh).astype(BF16)
    row8 = lax.broadcasted_iota(jnp.int32, (SUBLANES, nc), 0)
    has_prev = t_idx > 0
    has_next = t_idx < nt - 1

    def up_proj(c):
        return [jnp.dot(h, wu_ref[:, base:base + nc], preferred_element_type=F32)
                for base in (c * nc, d_ff + c * nc)]

    def conv_act(c, ups):
        us = []
        for base, ua in zip((c * nc, d_ff + c * nc), ups):
            up = ua[:tm]
            prev_row = jnp.where(has_prev, ua[tm + HALO - 1:tm + HALO, :], 0.0)
            next_row = jnp.where(has_next, ua[tm + HALO:tm + HALO + 1, :], 0.0)
            dn = pltpu.roll(up, 1, 0)
            um = jnp.concatenate([jnp.where(row8 == 0, prev_row, dn[:SUBLANES]), dn[SUBLANES:]], axis=0)
            nx = pltpu.roll(up, tm - 1, 0)
            upn = jnp.concatenate([nx[:tm - SUBLANES],
                                   jnp.where(row8 == SUBLANES - 1, next_row, nx[tm - SUBLANES:])], axis=0)
            w = cw_ref[:, base:base + nc]
            us.append(um * w[0:1, :] + up * w[1:2, :] + upn * w[2:3, :] + cb_ref[:, base:base + nc])
        return (_silu(us[0]) * us[1]).astype(BF16)

    n_chunks = d_ff // nc
    acts = []
    ups = up_proj(0)
    for c in range(n_chunks):
        ups_next = up_proj(c + 1) if c + 1 < n_chunks else None
        acts.append(conv_act(c, ups))
        ups = ups_next
    acc = jnp.dot(jnp.concatenate(acts, axis=1), wd_ref[...], preferred_element_type=F32)
    x2 = x + g2_ref[...] * acc
    o_ref[...] = _rms(x2, fg_ref[...])


def _mix_ffn_call(x, oa, of, ob, z, g1, ag, gg, w_out,
                  sh2, sc2, g2, ng, w_up, conv_w, conv_b, w_down, fg, *, tm, nc):
    b, t, d = x.shape
    d_ff = w_down.shape[0]
    nb8 = t // HALO
    tm8 = tm // HALO
    mod = lambda i, j: (i, 0, 0)
    const2 = lambda i, j: (0, 0)

    def with_halo(width):
        return [pl.BlockSpec((None, tm, width), lambda i, j: (i, j, 0)),
                pl.BlockSpec((None, HALO, width),
                             lambda i, j: (i, jnp.maximum(j * tm8 - 1, 0), 0)),
                pl.BlockSpec((None, HALO, width),
                             lambda i, j: (i, jnp.minimum((j + 1) * tm8, nb8 - 1), 0))]

    def resident(shape):
        return pl.BlockSpec(shape, const2, pipeline_mode=pl.Buffered(1))

    in_specs = (with_halo(d) + with_halo(ATTN_W) + with_halo(GDN_W) + with_halo(GDN_W)
                + with_halo(GDN_W)
                + [pl.BlockSpec((None, 1, d), mod),
                   pl.BlockSpec((1, ATTN_W), const2),
                   pl.BlockSpec((1, DK), const2),
                   resident((ATTN_W + GDN_W, d)),
                   pl.BlockSpec((None, 1, d), mod),
                   pl.BlockSpec((None, 1, d), mod),
                   pl.BlockSpec((None, 1, d), mod),
                   pl.BlockSpec((1, d), const2),
                   resident((d, 2 * d_ff)),
                   pl.BlockSpec((3, 2 * d_ff), const2),
                   pl.BlockSpec((1, 2 * d_ff), const2),
                   resident((d_ff, d)),
                   pl.BlockSpec((1, d), const2)])
    return pl.pallas_call(
        functools.partial(_mix_ffn_kernel, tm=tm, nc=nc),
        grid=(b, t // tm),
        in_specs=in_specs,
        out_specs=pl.BlockSpec((None, tm, d), lambda i, j: (i, j, 0)),
        out_shape=jax.ShapeDtypeStruct((b, t, d), F32),
        compiler_params=_cparams(("parallel", "parallel")),
        name="mix_ffn",
    )(x, x, x, oa, oa, oa, of, of, of, ob, ob, ob, z, z, z, g1, ag, gg, w_out,
      sh2, sc2, g2, ng, w_up, conv_w, conv_b, w_down, fg)


def _rope_tables(t):
    rows = t // GRID_W
    r = jnp.repeat(jnp.arange(rows, dtype=F32), GRID_W)
    c = jnp.tile(jnp.arange(GRID_W, dtype=F32), rows)
    pairs = HD // 4
    inv_freq = ROPE_THETA ** (-jnp.arange(pairs, dtype=F32) / pairs)
    ang = jnp.concatenate([r[:, None] * inv_freq, c[:, None] * inv_freq], axis=-1)
    cos, sin = jnp.cos(ang), jnp.sin(ang)
    cos128 = jnp.tile(cos, (1, LANES // (HD // 2)))
    sin128 = jnp.tile(jnp.concatenate([-sin, sin], axis=-1), (1, LANES // HD))
    return cos128, sin128


def _pick(n, candidates):
    for c in candidates:
        if n % c == 0:
            return c
    raise ValueError(f"no tile size in {candidates} divides {n}")


def kernel(x, c, ctx, c_ctx, w_mod, b_mod, norm1_g, w_in, q_norm_g, k_norm_g, attn_out_g,
           conv_qkv_w, a_log_f, a_log_b, dt_bias_f, dt_bias_b, gdn_norm_g, w_out, norm2_g,
           w_up, ffn_conv_w, ffn_conv_b, w_down, final_norm_g):
    depth = w_mod.shape[0]
    assert depth == 1, "only the single-layer configuration is implemented"
    b, t, d = x.shape
    tc = ctx.shape[1]
    assert t % CHUNK == 0 and tc % CHUNK == 0 and t % GRID_W == 0

    rows = -(-(b + 1) // SUBLANES) * SUBLANES
    cc = jnp.zeros((rows, d), F32).at[:b].set(c).at[b].set(c_ctx)
    mod = _mod_call(cc, w_mod[0], b_mod[0]).reshape(rows, N_MOD, d)
    lat = [mod[:b, i][:, None, :] for i in range(N_MOD)]
    cm = [mod[b:b + 1, i][:, None, :] for i in range(2)]

    w_pack = jnp.pad(w_in[0], ((0, 0), (0, PACK_W - w_in.shape[-1]))).astype(BF16)
    qg = jnp.tile(q_norm_g[0], LANES // HD)[None, :]
    kg = jnp.tile(k_norm_g[0], LANES // HD)[None, :]
    cos, sin = _rope_tables(t)
    g1n = norm1_g[0][None, :]

    tm = _pick(t, (512, 256, 128))
    q, kl, vl, gdl, z, abl = _inproj_call(x, lat[0], lat[1], g1n, w_pack, qg, kg, cos, sin,
                                          latent=True, tm=tm)
    kc, vc, gdc, abc = _inproj_call(ctx, cm[0], cm[1], g1n, w_pack, None, kg, None, None,
                                    latent=False, tm=_pick(tc, (256, 128)))

    o_attn = _attn_call(q, kc, vc, kl, vl, tq=_pick(t, (1024, 512, 256, 128)), tk=_pick(t, (512, 256, 128)))

    par = jnp.zeros((SUBLANES, LANES), F32)
    par = par.at[0, :2 * H_GDN].set(jnp.concatenate([a_log_f[0], a_log_b[0]]))
    par = par.at[1, :2 * H_GDN].set(jnp.concatenate([dt_bias_f[0], dt_bias_b[0]]))
    cw = conv_qkv_w[0]
    s0 = jnp.zeros((b, 2, H_GDN, DK, DK), F32)
    (s_ctx,) = _gdn_call(gdc, cw, abc, par, s0, tb=_pick(tc, (256, 128)), nb=tc // CHUNK,
                         with_out=False)
    o_f, o_b, _ = _gdn_call(gdl, cw, abl, par, s_ctx, tb=_pick(t, (512, 256, 128)),
                            nb=_pick(t // CHUNK, (4, 2, 1)), with_out=True)

    return _mix_ffn_call(x, o_attn, o_f, o_b, z, lat[2], attn_out_g[0][None, :],
                         gdn_norm_g[0][None, :], w_out[0].astype(BF16),
                         lat[3], lat[4], lat[5], norm2_g[0][None, :], w_up[0].astype(BF16),
                         ffn_conv_w[0], ffn_conv_b[0][None, :], w_down[0].astype(BF16),
                         final_norm_g[None, :], tm=tm, nc=256)
```

```python
import functools

import jax
import jax.numpy as jnp
import numpy as np
from jax import lax
from jax.experimental import pallas as pl
from jax.experimental.pallas import tpu as pltpu

F32 = jnp.float32
BF16 = jnp.bfloat16

GRID_W = 64
H_ATTN = 8
KV_ATTN = 2
HD = 64
GQA = H_ATTN // KV_ATTN
ATTN_W = H_ATTN * HD
KV_W = KV_ATTN * HD
H_GDN = 4
DK = 128
GDN_QK = H_GDN * DK
GDN_QKV_W = 3 * GDN_QK
GDN_W = H_GDN * DK
N_MOD = 6
ROPE_THETA = 10000.0
EPS = 1e-6
Q_SCALE = HD ** -0.5 * float(np.log2(np.e))
SOFTMAX_SAFE_BOUND = 60.0

LANES = 128
SUBLANES = 8
HALO = 16
VMEM_LIMIT_BYTES = 56 * 1024 * 1024
GDN_VMEM_LIMIT_BYTES = 62 * 1024 * 1024

CHUNK = 128
GDN_GROUP = 2

COL_Q = 0
COL_K = COL_Q + ATTN_W
COL_V = COL_K + KV_W
COL_G = COL_V + KV_W
COL_Z = COL_G + GDN_QKV_W
COL_AB = COL_Z + GDN_W
PACK_W = COL_AB + LANES


def _cparams(sem, vmem_limit_bytes=VMEM_LIMIT_BYTES):
    return pltpu.CompilerParams(dimension_semantics=sem, vmem_limit_bytes=vmem_limit_bytes)


def _silu(x):
    return x * jax.nn.sigmoid(x)


def _rms(x, gain):
    return x * lax.rsqrt(jnp.mean(x * x, axis=-1, keepdims=True) + EPS) * gain


def _mod_kernel(c_ref, w_ref, b_ref, o_ref):
    s = _silu(c_ref[...])
    o_ref[...] = jnp.dot(s, w_ref[...], preferred_element_type=F32,
                         precision=lax.Precision.HIGHEST) + b_ref[...]


def _mod_call(cc, w_mod, b_mod):
    rows, d = cc.shape
    n = w_mod.shape[1]
    bn = 1024
    return pl.pallas_call(
        _mod_kernel,
        grid=(n // bn,),
        in_specs=[pl.BlockSpec((rows, d), lambda j: (0, 0)),
                  pl.BlockSpec((d, bn), lambda j: (0, j)),
                  pl.BlockSpec((1, bn), lambda j: (0, j))],
        out_specs=pl.BlockSpec((rows, bn), lambda j: (0, j)),
        out_shape=jax.ShapeDtypeStruct((rows, n), F32),
        compiler_params=_cparams(("arbitrary",)),
        name="mod",
    )(cc, w_mod, b_mod.reshape(1, n))


def _low_half(shape):
    return (lax.broadcasted_iota(jnp.int32, shape, 1) & HD) == 0


def _head_norm(t, gain):
    lo = _low_half(t.shape)
    sq = t * t
    s_lo = jnp.sum(jnp.where(lo, sq, 0.0), axis=-1, keepdims=True)
    s_hi = jnp.sum(jnp.where(lo, 0.0, sq), axis=-1, keepdims=True)
    ms = jnp.where(lo, s_lo, s_hi) * (1.0 / HD)
    return t * lax.rsqrt(ms + EPS) * gain


def _rope(t, cos, sin_signed):
    first = (lax.broadcasted_iota(jnp.int32, t.shape, 1) & (HD // 2)) == 0
    partner = jnp.where(first, pltpu.roll(t, LANES - HD // 2, 1), pltpu.roll(t, HD // 2, 1))
    return t * cos + partner * sin_signed


def _inproj_kernel(*refs, latent):
    if latent:
        (x_ref, sh_ref, sc_ref, g_ref, w_ref, qg_ref, kg_ref, cos_ref, sin_ref,
         q_ref, k_ref, v_ref, gd_ref, z_ref, ab_ref) = refs
    else:
        (x_ref, sh_ref, sc_ref, g_ref, w_ref, kg_ref,
         k_ref, v_ref, gd_ref, ab_ref) = refs
    x = x_ref[...]
    h = _rms(x, g_ref[...]) * (1.0 + sc_ref[...]) + sh_ref[...]
    y = jnp.dot(h.astype(BF16), w_ref[...], preferred_element_type=F32)

    kt = _head_norm(y[:, COL_K:COL_K + KV_W], kg_ref[...])
    if latent:
        cos = cos_ref[...]
        sin = sin_ref[...]
        kt = _rope(kt, cos, sin)
        for i in range(ATTN_W // LANES):
            qt = y[:, COL_Q + i * LANES:COL_Q + (i + 1) * LANES]
            qt = _rope(_head_norm(qt, qg_ref[...]), cos, sin) * Q_SCALE
            q_ref[:, i * LANES:(i + 1) * LANES] = qt.astype(BF16)
        z_ref[...] = y[:, COL_Z:COL_Z + GDN_W].astype(BF16)
    k_ref[...] = kt.astype(BF16)
    v_ref[...] = y[:, COL_V:COL_V + KV_W].astype(BF16)
    gd_ref[...] = y[:, COL_G:COL_G + GDN_QKV_W].astype(BF16)
    ab_ref[...] = y[:, COL_AB:COL_AB + LANES]


def _inproj_call(x, shift, scale, gain, w_pack, qg, kg, cos, sin, *, latent, tm):
    b, t, d = x.shape
    nt = t // tm
    per_batch = shift.shape[0] == b
    mod_map = (lambda i, j: (i, 0, 0)) if per_batch else (lambda i, j: (0, 0, 0))
    row = lambda i, j: (i, j, 0)
    const2 = lambda i, j: (0, 0)
    in_specs = [pl.BlockSpec((None, tm, d), row),
                pl.BlockSpec((None, 1, d), mod_map),
                pl.BlockSpec((None, 1, d), mod_map),
                pl.BlockSpec((1, d), const2),
                pl.BlockSpec((d, PACK_W), const2)]
    args = [x, shift, scale, gain, w_pack]
    if latent:
        in_specs += [pl.BlockSpec((1, LANES), const2), pl.BlockSpec((1, LANES), const2),
                     pl.BlockSpec((tm, LANES), lambda i, j: (j, 0)),
                     pl.BlockSpec((tm, LANES), lambda i, j: (j, 0))]
        args += [qg, kg, cos, sin]
    else:
        in_specs += [pl.BlockSpec((1, LANES), const2)]
        args += [kg]
    kv_spec = pl.BlockSpec((None, tm, KV_W), row)
    kv_shape = jax.ShapeDtypeStruct((b, t, KV_W), BF16)
    out_specs = [kv_spec, kv_spec, pl.BlockSpec((None, tm, GDN_QKV_W), row)]
    out_shape = [kv_shape, kv_shape, jax.ShapeDtypeStruct((b, t, GDN_QKV_W), BF16)]
    if latent:
        out_specs = [pl.BlockSpec((None, tm, ATTN_W), row)] + out_specs + [pl.BlockSpec((None, tm, GDN_W), row)]
        out_shape = [jax.ShapeDtypeStruct((b, t, ATTN_W), BF16)] + out_shape + [jax.ShapeDtypeStruct((b, t, GDN_W), BF16)]
    out_specs.append(pl.BlockSpec((None, tm, LANES), row))
    out_shape.append(jax.ShapeDtypeStruct((b, t, LANES), F32))
    return pl.pallas_call(
        functools.partial(_inproj_kernel, latent=latent),
        grid=(b, nt),
        in_specs=in_specs,
        out_specs=out_specs,
        out_shape=out_shape,
        compiler_params=_cparams(("parallel", "parallel")),
        name="inproj_lat" if latent else "inproj_ctx",
    )(*args)


def _attn_kernel(q_ref, kc_ref, vc_ref, kl_ref, vl_ref, o_ref, vt_ref, kmax2_ref, *, tq, tk):
    j = pl.program_id(1)
    tc = kc_ref.shape[0]
    n_lat = kl_ref.shape[0]

    @pl.when(pl.program_id(2) == 0)
    def _():
        def put(off, v):
            vt = v.astype(F32).T
            other = (lax.broadcasted_iota(jnp.int32, vt.shape, 0) // HD + j) & 1
            vt_ref[:, off:off + v.shape[0]] = jnp.where(other == 1, 1.0, vt).astype(BF16)

        def key_norm2(k):
            kf = k.astype(F32)
            mine = (lax.broadcasted_iota(jnp.int32, kf.shape, 1) // HD) == j
            n2 = jnp.sum(jnp.where(mine, kf * kf, 0.0), axis=1, keepdims=True)
            return jnp.max(n2, axis=0, keepdims=True)

        put(0, vc_ref[...])
        kmax2 = key_norm2(kc_ref[...])
        for c in range(n_lat // tk):
            put(tc + c * tk, vl_ref[c * tk:(c + 1) * tk, :])
            kmax2 = jnp.maximum(kmax2, key_norm2(kl_ref[c * tk:(c + 1) * tk, :]))
        kmax2_ref[...] = jnp.broadcast_to(kmax2, kmax2_ref.shape)

    qt = q_ref[...].astype(F32).T
    parts = []
    for g in range(GQA):
        qg = qt[g * HD:(g + 1) * HD, :]
        parts.append(jnp.concatenate([jnp.where(j == 0, qg, 0.0), jnp.where(j == 1, qg, 0.0)], axis=0))
    wq = jnp.concatenate(parts, axis=1).astype(BF16)
    nq = wq.shape[1]

    segs = [(kc_ref, 0, 0, tc)] + [(kl_ref, c * tk, tc + c * tk, tk) for c in range(n_lat // tk)]

    def scores(seg):
        k_ref, start, _, size = seg
        return jnp.dot(k_ref[start:start + size, :], wq, preferred_element_type=F32)

    def finish(acc):
        ot = jnp.where(j == 0, acc[:HD], acc[HD:])
        ot = ot / jnp.where(j == 0, acc[HD:HD + 1], acc[0:1])
        og = jnp.concatenate([ot[:, g * tq:(g + 1) * tq] for g in range(GQA)], axis=0)
        o_ref[...] = og.T.astype(BF16)

    wqf = wq.astype(F32)
    bound = jnp.sqrt(jnp.sum(wqf * wqf, axis=0, keepdims=True) * kmax2_ref[0:1, 0:1])
    bounded = jnp.max(bound) <= SOFTMAX_SAFE_BOUND

    @pl.when(bounded)
    def _():
        acc = jnp.zeros((KV_W, nq), F32)
        s_next = scores(segs[0])
        for idx, (_, _, voff, size) in enumerate(segs):
            s = s_next
            if idx + 1 < len(segs):
                s_next = scores(segs[idx + 1])
            p = jnp.exp2(s - bound).astype(BF16)
            acc = acc + jnp.dot(vt_ref[:, voff:voff + size], p, preferred_element_type=F32)
        finish(acc)

    @pl.when(jnp.logical_not(bounded))
    def _():
        m = jnp.full((1, nq), -jnp.inf, F32)
        acc = jnp.zeros((KV_W, nq), F32)
        s_next = scores(segs[0])
        for idx, (_, _, voff, size) in enumerate(segs):
            s = s_next
            if idx + 1 < len(segs):
                s_next = scores(segs[idx + 1])
            m_new = jnp.maximum(m, jnp.max(s, axis=0, keepdims=True))
            alpha = jnp.exp2(m - m_new)
            p = jnp.exp2(s - m_new).astype(BF16)
            acc = alpha * acc + jnp.dot(vt_ref[:, voff:voff + size], p,
                                        preferred_element_type=F32)
            m = m_new
        finish(acc)


def _attn_call(q, kc, vc, kl, vl, *, tq, tk):
    b, t, _ = q.shape
    tc = kc.shape[1]
    gw = GQA * HD
    kv_c = pl.BlockSpec((None, tc, KV_W), lambda i, j, n: (i, 0, 0))
    kv_l = pl.BlockSpec((None, t, KV_W), lambda i, j, n: (i, 0, 0))
    qo = pl.BlockSpec((None, tq, gw), lambda i, j, n: (i, n, j))
    return pl.pallas_call(
        functools.partial(_attn_kernel, tq=tq, tk=tk),
        grid=(b, KV_ATTN, t // tq),
        in_specs=[qo, kv_c, kv_c, kv_l, kv_l],
        out_specs=qo,
        out_shape=jax.ShapeDtypeStruct((b, t, ATTN_W), BF16),
        scratch_shapes=[pltpu.VMEM((KV_W, tc + t), BF16), pltpu.VMEM((SUBLANES, LANES), F32)],
        compiler_params=_cparams(("parallel", "arbitrary", "arbitrary")),
        name="attn",
    )(q, kc, vc, kl, vl)


def _gdn_prep(x_ref, xp_ref, xn_ref, cw_ref, ab_ref, par_ref,
              wq_ref, ik_ref, u_ref, dl_ref, *, tb, t_idx, nt):
    c = CHUNK
    chunk0 = t_idx * (tb // c)

    def rows_of(n):
        return pl.ds(pl.multiple_of((chunk0 + n) * c, c), c)

    x = x_ref[...].astype(F32)
    row = lax.broadcasted_iota(jnp.int32, x.shape, 0)
    prev_row = jnp.where(t_idx > 0, xp_ref[...].astype(F32)[HALO - 1:HALO, :], 0.0)
    next_row = jnp.where(t_idx < nt - 1, xn_ref[...].astype(F32)[0:1, :], 0.0)
    xm = jnp.where(row == 0, prev_row, pltpu.roll(x, 1, 0))
    xp = jnp.where(row == tb - 1, next_row, pltpu.roll(x, tb - 1, 0))
    w = cw_ref[...]

    def l2n(a):
        return a * lax.rsqrt(jnp.sum(a * a, axis=-1, keepdims=True) + EPS)

    ab = ab_ref[...]
    par = par_ref[...]
    g_all = -jnp.exp(par[0:1, :]) * jax.nn.softplus(ab + par[1:2, :])
    beta_all = jax.nn.sigmoid(ab)

    ri = lax.broadcasted_iota(jnp.int32, (c, c), 0)
    ci = lax.broadcasted_iota(jnp.int32, (c, c), 1)
    eye = (ri == ci).astype(F32)
    xr = ri ^ ci
    lane = lax.broadcasted_iota(jnp.int32, (c, LANES), 1)
    tri_f = (ri >= ci).astype(F32)
    tri_b = (ri <= ci).astype(F32)

    def blockdiag(xy):
        z = jnp.zeros((c, c), xy.dtype)
        return jnp.concatenate([jnp.concatenate([xy[:, :c], z], axis=1),
                                jnp.concatenate([z, xy[:, c:]], axis=1)], axis=0)

    def prep_chunk(n, chains):
        sl = slice(n * c, (n + 1) * c)
        y = _silu(xm[sl] * w[0:1, :] + x[sl] * w[1:2, :] + xp[sl] * w[2:3, :])
        qs = [l2n(y[:, h * DK:(h + 1) * DK]) * (DK ** -0.5) for h in range(H_GDN)]
        ks = [l2n(y[:, GDN_QK + h * DK:GDN_QK + (h + 1) * DK]) for h in range(H_GDN)]
        vs = [y[:, 2 * GDN_QK + h * DK:2 * GDN_QK + (h + 1) * DK] for h in range(H_GDN)]
        yield
        g_c = g_all[sl]
        beta_c = beta_all[sl]
        cs_f = jnp.dot(tri_f, g_c, preferred_element_type=F32, precision=lax.Precision.HIGHEST)
        cs_b = jnp.dot(tri_b, g_c, preferred_element_type=F32, precision=lax.Precision.HIGHEST)
        gc_all = jnp.where((lane & H_GDN) == 0, cs_f, cs_b)
        gc_t = gc_all.T
        kkqk_h = []
        for h0 in range(0, H_GDN, 2):
            kq2 = jnp.concatenate(
                [jnp.concatenate([ks[h], qs[h]], axis=0) for h in (h0, h0 + 1)],
                axis=1).astype(BF16)
            k2 = blockdiag(jnp.concatenate([ks[h0], ks[h0 + 1]], axis=1).astype(BF16))
            res = lax.dot_general(kq2, k2, (((1,), (1,)), ((), ())), preferred_element_type=F32)
            kkqk_h += [res[:, :c], res[:, c:]]
        yield
        for h in range(H_GDN):
            qc, kc, vc = qs[h], ks[h], vs[h]
            kk, qk = kkqk_h[h][:c], kkqk_h[h][c:]
            k_t = kc.T
            for d in range(2):
                gi = d * H_GDN + h
                bi = 2 * H_GDN + gi
                gcol = gc_all[:, gi:gi + 1]
                bcol = beta_c[:, bi:bi + 1]
                grow = gc_t[gi:gi + 1, :]
                incl = (ri >= ci) if d == 0 else (ri <= ci)
                strict = (ri > ci) if d == 0 else (ri < ci)
                dec = jnp.exp(jnp.where(incl, gcol - grow, -jnp.inf))
                a_mat = jnp.where(strict, kk * bcol * dec, 0.0)
                egc = jnp.exp(gcol)
                rhs = jnp.concatenate([vc * bcol, kc * (bcol * egc)], axis=1).astype(BF16)
                glast = gcol[c - 1:c, :] if d == 0 else gcol[0:1, :]
                wq_ref[d, h, chunk0 + n, c:2 * c, :] = (qc * egc).astype(BF16)
                ik_ref[d, h, chunk0 + n, 0:c, :] = (qk * dec).astype(BF16)
                ik_ref[d, h, chunk0 + n, c:c + DK, :] = (k_t * jnp.exp(glast - grow)).astype(BF16)
                dl_ref[d, h, chunk0 + n] = jnp.broadcast_to(jnp.exp(glast), (SUBLANES, LANES))
                chains.append((d, h, n, a_mat, rhs))
            yield

    xr2 = jnp.concatenate([xr, xr], axis=1)
    eye2 = jnp.concatenate([eye, eye], axis=1)

    def solve(chains):
        a_pairs = [jnp.concatenate([chains[i][3], chains[i + 1][3]], axis=1)
                   for i in range(0, len(chains), 2)]
        inv_pairs = [eye2 - jnp.where((xr2 >> 1) == 0, ap, 0.0) for ap in a_pairs]
        for lvl in range(1, int(np.log2(c))):
            sel = ((xr2 >> (lvl + 1)) == 0) & ((xr2 >> lvl) != 0)
            invb = [ip.astype(BF16) for ip in inv_pairs]
            mts = [jnp.dot(jnp.where(sel, ap, 0.0).astype(BF16), blockdiag(ib),
                           preferred_element_type=F32) for ap, ib in zip(a_pairs, invb)]
            yield
            inv_pairs = [ip - jnp.dot(ib, blockdiag(mt.astype(BF16)), preferred_element_type=F32)
                         for ip, ib, mt in zip(inv_pairs, invb, mts)]
            yield
        invs = [ip[:, s * c:(s + 1) * c] for ip in inv_pairs for s in range(2)]
        for (d, h, n, _, rhs), inv in zip(chains, invs):
            uw = jnp.dot(inv.astype(BF16), rhs, preferred_element_type=F32)
            u_ref[d, h, rows_of(n), :] = uw[:, :DK].astype(BF16)
            wq_ref[d, h, chunk0 + n, 0:c, :] = uw[:, DK:].astype(BF16)

    def prep_group(chunk_ids, chains):
        for n in chunk_ids:
            yield from prep_chunk(n, chains)

    n_chunks = tb // c
    group = min(GDN_GROUP, n_chunks)
    pending = []
    for _ in prep_group(range(group), pending):
        pass
    for g0 in range(group, n_chunks + group, group):
        current, pending = pending, []
        nxt = prep_group(range(g0, min(g0 + group, n_chunks)), pending)
        for _ in solve(current):
            next(nxt, None)
        for _ in nxt:
            pass


def _gdn_scan(wq_ref, ik_ref, u_ref, dl_ref, s_ref, of_ref, ob_ref, *, nb, j, ns):
    c = CHUNK
    base = (j * nb, (ns - 1 - j) * nb)
    chains = [(d, h) for d in range(2) for h in range(H_GDN)]
    for n in range(nb):
        cns = [n if d == 0 else nb - 1 - n for d, _ in chains]
        ss = [s_ref[d, h] for d, h in chains]
        r1s = [jnp.dot(wq_ref[d, h, base[d] + cn], s.astype(BF16), preferred_element_type=F32)
               for (d, h), cn, s in zip(chains, cns, ss)]
        vns = [u_ref[d, h, pl.ds(pl.multiple_of((base[d] + cn) * c, c), c), :].astype(F32) - r1[:c]
               for (d, h), cn, r1 in zip(chains, cns, r1s)]
        r2s = [jnp.dot(ik_ref[d, h, base[d] + cn], vn.astype(BF16), preferred_element_type=F32)
               for (d, h), cn, vn in zip(chains, cns, vns)]
        for (d, h), cn, s, r1, r2 in zip(chains, cns, ss, r1s, r2s):
            if of_ref is not None:
                o_ref = of_ref if d == 0 else ob_ref
                o_ref[cn * c:(cn + 1) * c, h * DK:(h + 1) * DK] = (r1[c:] + r2[:c]).astype(BF16)
            s_ref[d, h] = s * dl_ref[d, h, base[d] + cn, 0:1, 0:1] + r2[c:]


def _gdn_kernel(x_ref, xp_ref, xn_ref, cw_ref, ab_ref, par_ref, s0_ref, *refs,
                tb, nb, n_prep, n_scan, with_out):
    if with_out:
        of_ref, ob_ref, sfin_ref, wq_s, ik_s, u_s, dl_s, s_ref = refs
    else:
        sfin_ref, wq_s, ik_s, u_s, dl_s, s_ref = refs
        of_ref = ob_ref = None
    step = pl.program_id(1)

    @pl.when(step < n_prep)
    def _():
        _gdn_prep(x_ref, xp_ref, xn_ref, cw_ref, ab_ref, par_ref, wq_s, ik_s, u_s, dl_s,
                  tb=tb, t_idx=step, nt=n_prep)

    @pl.when(step == n_prep)
    def _():
        s_ref[...] = s0_ref[...]

    @pl.when(step >= n_prep)
    def _():
        _gdn_scan(wq_s, ik_s, u_s, dl_s, s_ref, of_ref, ob_ref, nb=nb, j=step - n_prep, ns=n_scan)

    @pl.when(step == n_prep + n_scan - 1)
    def _():
        sfin_ref[...] = s_ref[...]


def _gdn_call(gd, conv_w, ab, par, s0, *, tb, nb, with_out):
    b, t, _ = gd.shape
    c = CHUNK
    nchunk = t // c
    n_prep = t // tb
    n_scan = nchunk // nb
    nb8 = t // HALO
    tb8 = tb // HALO
    blk = lambda j: jnp.minimum(j, n_prep - 1)
    sj = lambda j: jnp.maximum(j - n_prep, 0)
    in_specs = [pl.BlockSpec((None, tb, GDN_QKV_W), lambda i, j: (i, blk(j), 0)),
                pl.BlockSpec((None, HALO, GDN_QKV_W),
                             lambda i, j: (i, jnp.maximum(blk(j) * tb8 - 1, 0), 0)),
                pl.BlockSpec((None, HALO, GDN_QKV_W),
                             lambda i, j: (i, jnp.minimum((blk(j) + 1) * tb8, nb8 - 1), 0)),
                pl.BlockSpec((3, GDN_QKV_W), lambda i, j: (0, 0)),
                pl.BlockSpec((None, tb, LANES), lambda i, j: (i, blk(j), 0)),
                pl.BlockSpec((SUBLANES, LANES), lambda i, j: (0, 0)),
                pl.BlockSpec((None, 2, H_GDN, DK, DK), lambda i, j: (i, 0, 0, 0, 0))]
    out_specs = []
    out_shape = []
    if with_out:
        out_specs += [pl.BlockSpec((None, nb * c, GDN_W), lambda i, j: (i, sj(j), 0)),
                      pl.BlockSpec((None, nb * c, GDN_W), lambda i, j: (i, n_scan - 1 - sj(j), 0))]
        out_shape += [jax.ShapeDtypeStruct((b, t, GDN_W), BF16)] * 2
    out_specs.append(pl.BlockSpec((None, 2, H_GDN, DK, DK), lambda i, j: (i, 0, 0, 0, 0)))
    out_shape.append(jax.ShapeDtypeStruct((b, 2, H_GDN, DK, DK), F32))
    return pl.pallas_call(
        functools.partial(_gdn_kernel, tb=tb, nb=nb, n_prep=n_prep, n_scan=n_scan,
                          with_out=with_out),
        grid=(b, n_prep + n_scan),
        in_specs=in_specs,
        out_specs=out_specs,
        out_shape=out_shape,
        scratch_shapes=[pltpu.VMEM((2, H_GDN, nchunk, 2 * c, DK), BF16),
                        pltpu.VMEM((2, H_GDN, nchunk, c + DK, c), BF16),
                        pltpu.VMEM((2, H_GDN, t, DK), BF16),
                        pltpu.VMEM((2, H_GDN, nchunk, SUBLANES, LANES), F32),
                        pltpu.VMEM((2, H_GDN, DK, DK), F32)],
        compiler_params=_cparams(("parallel", "arbitrary"), GDN_VMEM_LIMIT_BYTES),
        name="gdn" if with_out else "gdn_ctx",
    )(gd, gd, gd, conv_w, ab, par, s0)


def _mix_ffn_kernel(x_ref, xp_ref, xn_ref, oa_ref, oap_ref, oan_ref, of_ref, ofp_ref, ofn_ref,
                    ob_ref, obp_ref, obn_ref, z_ref, zp_ref, zn_ref,
                    g1_ref, ag_ref, gg_ref, wo_ref,
                    sh_ref, sc_ref, g2_ref, ng_ref, wu_ref, cw_ref, cb_ref, wd_ref, fg_ref,
                    o_ref, *, tm, nc):
    t_idx = pl.program_id(1)
    nt = pl.num_programs(1)
    d_ff = wd_ref.shape[0]

    def rows3(a, ap, an):
        return jnp.concatenate([a[...], ap[...], an[...]], axis=0).astype(F32)

    oa = _rms(rows3(oa_ref, oap_ref, oan_ref), ag_ref[...])
    og = rows3(of_ref, ofp_ref, ofn_ref) + rows3(ob_ref, obp_ref, obn_ref)
    z = rows3(z_ref, zp_ref, zn_ref)
    gg = gg_ref[...]
    parts = [oa.astype(BF16)]
    for hd in range(H_GDN):
        sl = slice(hd * DK, (hd + 1) * DK)
        parts.append((_rms(og[:, sl], gg) * _silu(z[:, sl])).astype(BF16))
    lat = jnp.dot(jnp.concatenate(parts, axis=1), wo_ref[...], preferred_element_type=F32)
    xa = rows3(x_ref, xp_ref, xn_ref) + g1_ref[...] * lat
    x = xa[:tm]

    ng, sc, sh = ng_ref[...], sc_ref[...], sh_ref[...]
    h = (_rms(xa, ng) * (1.0 + sc) + sh).astype(BF16)
    row8 = lax.broadcasted_iota(jnp.int32, (SUBLANES, nc), 0)
    has_prev = t_idx > 0
    has_next = t_idx < nt - 1

    def up_proj(c):
        return [jnp.dot(h, wu_ref[:, base:base + nc], preferred_element_type=F32)
                for base in (c * nc, d_ff + c * nc)]

    def conv_act(c, ups):
        us = []
        for base, ua in zip((c * nc, d_ff + c * nc), ups):
            up = ua[:tm]
            prev_row = jnp.where(has_prev, ua[tm + HALO - 1:tm + HALO, :], 0.0)
            next_row = jnp.where(has_next, ua[tm + HALO:tm + HALO + 1, :], 0.0)
            dn = pltpu.roll(up, 1, 0)
            um = jnp.concatenate([jnp.where(row8 == 0, prev_row, dn[:SUBLANES]), dn[SUBLANES:]], axis=0)
            nx = pltpu.roll(up, tm - 1, 0)
            upn = jnp.concatenate([nx[:tm - SUBLANES],
                                   jnp.where(row8 == SUBLANES - 1, next_row, nx[tm - SUBLANES:])], axis=0)
            w = cw_ref[:, base:base + nc]
            us.append(um * w[0:1, :] + up * w[1:2, :] + upn * w[2:3, :] + cb_ref[:, base:base + nc])
        return (_silu(us[0]) * us[1]).astype(BF16)

    n_chunks = d_ff // nc
    acts = []
    ups = up_proj(0)
    for c in range(n_chunks):
        ups_next = up_proj(c + 1) if c + 1 < n_chunks else None
        acts.append(conv_act(c, ups))
        ups = ups_next
    acc = jnp.dot(jnp.concatenate(acts, axis=1), wd_ref[...], preferred_element_type=F32)
    x2 = x + g2_ref[...] * acc
    o_ref[...] = _rms(x2, fg_ref[...])


def _mix_ffn_call(x, oa, of, ob, z, g1, ag, gg, w_out,
                  sh2, sc2, g2, ng, w_up, conv_w, conv_b, w_down, fg, *, tm, nc):
    b, t, d = x.shape
    d_ff = w_down.shape[0]
    nb8 = t // HALO
    tm8 = tm // HALO
    mod = lambda i, j: (i, 0, 0)
    const2 = lambda i, j: (0, 0)

    def with_halo(width):
        return [pl.BlockSpec((None, tm, width), lambda i, j: (i, j, 0)),
                pl.BlockSpec((None, HALO, width),
                             lambda i, j: (i, jnp.maximum(j * tm8 - 1, 0), 0)),
                pl.BlockSpec((None, HALO, width),
                             lambda i, j: (i, jnp.minimum((j + 1) * tm8, nb8 - 1), 0))]

    def resident(shape):
        return pl.BlockSpec(shape, const2, pipeline_mode=pl.Buffered(1))

    in_specs = (with_halo(d) + with_halo(ATTN_W) + with_halo(GDN_W) + with_halo(GDN_W)
                + with_halo(GDN_W)
                + [pl.BlockSpec((None, 1, d), mod),
                   pl.BlockSpec((1, ATTN_W), const2),
                   pl.BlockSpec((1, DK), const2),
                   resident((ATTN_W + GDN_W, d)),
                   pl.BlockSpec((None, 1, d), mod),
                   pl.BlockSpec((None, 1, d), mod),
                   pl.BlockSpec((None, 1, d), mod),
                   pl.BlockSpec((1, d), const2),
                   resident((d, 2 * d_ff)),
                   pl.BlockSpec((3, 2 * d_ff), const2),
                   pl.BlockSpec((1, 2 * d_ff), const2),
                   resident((d_ff, d)),
                   pl.BlockSpec((1, d), const2)])
    return pl.pallas_call(
        functools.partial(_mix_ffn_kernel, tm=tm, nc=nc),
        grid=(b, t // tm),
        in_specs=in_specs,
        out_specs=pl.BlockSpec((None, tm, d), lambda i, j: (i, j, 0)),
        out_shape=jax.ShapeDtypeStruct((b, t, d), F32),
        compiler_params=_cparams(("parallel", "parallel")),
        name="mix_ffn",
    )(x, x, x, oa, oa, oa, of, of, of, ob, ob, ob, z, z, z, g1, ag, gg, w_out,
      sh2, sc2, g2, ng, w_up, conv_w, conv_b, w_down, fg)


def _rope_tables(t):
    rows = t // GRID_W
    r = jnp.repeat(jnp.arange(rows, dtype=F32), GRID_W)
    c = jnp.tile(jnp.arange(GRID_W, dtype=F32), rows)
    pairs = HD // 4
    inv_freq = ROPE_THETA ** (-jnp.arange(pairs, dtype=F32) / pairs)
    ang = jnp.concatenate([r[:, None] * inv_freq, c[:, None] * inv_freq], axis=-1)
    cos, sin = jnp.cos(ang), jnp.sin(ang)
    cos128 = jnp.tile(cos, (1, LANES // (HD // 2)))
    sin128 = jnp.tile(jnp.concatenate([-sin, sin], axis=-1), (1, LANES // HD))
    return cos128, sin128


def _pick(n, candidates):
    for c in candidates:
        if n % c == 0:
            return c
    raise ValueError(f"no tile size in {candidates} divides {n}")


def kernel(x, c, ctx, c_ctx, w_mod, b_mod, norm1_g, w_in, q_norm_g, k_norm_g, attn_out_g,
           conv_qkv_w, a_log_f, a_log_b, dt_bias_f, dt_bias_b, gdn_norm_g, w_out, norm2_g,
           w_up, ffn_conv_w, ffn_conv_b, w_down, final_norm_g):
    depth = w_mod.shape[0]
    assert depth == 1, "only the single-layer configuration is implemented"
    b, t, d = x.shape
    tc = ctx.shape[1]
    assert t % CHUNK == 0 and tc % CHUNK == 0 and t % GRID_W == 0

    rows = -(-(b + 1) // SUBLANES) * SUBLANES
    cc = jnp.zeros((rows, d), F32).at[:b].set(c).at[b].set(c_ctx)
    mod = _mod_call(cc, w_mod[0], b_mod[0]).reshape(rows, N_MOD, d)
    lat = [mod[:b, i][:, None, :] for i in range(N_MOD)]
    cm = [mod[b:b + 1, i][:, None, :] for i in range(2)]

    w_pack = jnp.pad(w_in[0], ((0, 0), (0, PACK_W - w_in.shape[-1]))).astype(BF16)
    qg = jnp.tile(q_norm_g[0], LANES // HD)[None, :]
    kg = jnp.tile(k_norm_g[0], LANES // HD)[None, :]
    cos, sin = _rope_tables(t)
    g1n = norm1_g[0][None, :]

    tm = _pick(t, (512, 256, 128))
    q, kl, vl, gdl, z, abl = _inproj_call(x, lat[0], lat[1], g1n, w_pack, qg, kg, cos, sin,
                                          latent=True, tm=tm)
    kc, vc, gdc, abc = _inproj_call(ctx, cm[0], cm[1], g1n, w_pack, None, kg, None, None,
                                    latent=False, tm=_pick(tc, (256, 128)))

    o_attn = _attn_call(q, kc, vc, kl, vl, tq=_pick(t, (1024, 512, 256, 128)), tk=_pick(t, (512, 256, 128)))

    par = jnp.zeros((SUBLANES, LANES), F32)
    par = par.at[0, :2 * H_GDN].set(jnp.concatenate([a_log_f[0], a_log_b[0]]))
    par = par.at[1, :2 * H_GDN].set(jnp.concatenate([dt_bias_f[0], dt_bias_b[0]]))
    cw = conv_qkv_w[0]
    s0 = jnp.zeros((b, 2, H_GDN, DK, DK), F32)
    (s_ctx,) = _gdn_call(gdc, cw, abc, par, s0, tb=_pick(tc, (256, 128)), nb=tc // CHUNK,
                         with_out=False)
    o_f, o_b, _ = _gdn_call(gdl, cw, abl, par, s_ctx, tb=_pick(t, (512, 256, 128)),
                            nb=_pick(t // CHUNK, (4, 2, 1)), with_out=True)

    return _mix_ffn_call(x, o_attn, o_f, o_b, z, lat[2], attn_out_g[0][None, :],
                         gdn_norm_g[0][None, :], w_out[0].astype(BF16),
                         lat[3], lat[4], lat[5], norm2_g[0][None, :], w_up[0].astype(BF16),
                         ffn_conv_w[0], ffn_conv_b[0][None, :], w_down[0].astype(BF16),
                         final_norm_g[None, :], tm=tm, nc=256)
```

```python
import functools

import jax
import jax.numpy as jnp
import numpy as np
from jax import lax
from jax.experimental import pallas as pl
from jax.experimental.pallas import tpu as pltpu

F32 = jnp.float32
BF16 = jnp.bfloat16

GRID_W = 64
H_ATTN = 8
KV_ATTN = 2
HD = 64
GQA = H_ATTN // KV_ATTN
ATTN_W = H_ATTN * HD
KV_W = KV_ATTN * HD
H_GDN = 4
DK = 128
GDN_QK = H_GDN * DK
GDN_QKV_W = 3 * GDN_QK
GDN_W = H_GDN * DK
N_MOD = 6
ROPE_THETA = 10000.0
EPS = 1e-6
Q_SCALE = HD ** -0.5 * float(np.log2(np.e))
SOFTMAX_SAFE_BOUND = 60.0

LANES = 128
SUBLANES = 8
HALO = 16
VMEM_LIMIT_BYTES = 56 * 1024 * 1024
GDN_VMEM_LIMIT_BYTES = 62 * 1024 * 1024

CHUNK = 128
GDN_GROUP = 2

COL_Q = 0
COL_K = COL_Q + ATTN_W
COL_V = COL_K + KV_W
COL_G = COL_V + KV_W
COL_Z = COL_G + GDN_QKV_W
COL_AB = COL_Z + GDN_W
PACK_W = COL_AB + LANES


def _cparams(sem, vmem_limit_bytes=VMEM_LIMIT_BYTES):
    return pltpu.CompilerParams(dimension_semantics=sem, vmem_limit_bytes=vmem_limit_bytes)


def _silu(x):
    return x * jax.nn.sigmoid(x)


def _rms(x, gain):
    return x * lax.rsqrt(jnp.mean(x * x, axis=-1, keepdims=True) + EPS) * gain


def _mod_kernel(c_ref, w_ref, b_ref, o_ref):
    s = _silu(c_ref[...])
    o_ref[...] = jnp.dot(s, w_ref[...], preferred_element_type=F32,
                         precision=lax.Precision.HIGHEST) + b_ref[...]


def _mod_call(cc, w_mod, b_mod):
    rows, d = cc.shape
    n = w_mod.shape[1]
    bn = 1024
    return pl.pallas_call(
        _mod_kernel,
        grid=(n // bn,),
        in_specs=[pl.BlockSpec((rows, d), lambda j: (0, 0)),
                  pl.BlockSpec((d, bn), lambda j: (0, j)),
                  pl.BlockSpec((1, bn), lambda j: (0, j))],
        out_specs=pl.BlockSpec((rows, bn), lambda j: (0, j)),
        out_shape=jax.ShapeDtypeStruct((rows, n), F32),
        compiler_params=_cparams(("arbitrary",)),
        name="mod",
    )(cc, w_mod, b_mod.reshape(1, n))


def _low_half(shape):
    return (lax.broadcasted_iota(jnp.int32, shape, 1) & HD) == 0


def _head_norm(t, gain):
    lo = _low_half(t.shape)
    sq = t * t
    s_lo = jnp.sum(jnp.where(lo, sq, 0.0), axis=-1, keepdims=True)
    s_hi = jnp.sum(jnp.where(lo, 0.0, sq), axis=-1, keepdims=True)
    ms = jnp.where(lo, s_lo, s_hi) * (1.0 / HD)
    return t * lax.rsqrt(ms + EPS) * gain


def _rope(t, cos, sin_signed):
    first = (lax.broadcasted_iota(jnp.int32, t.shape, 1) & (HD // 2)) == 0
    partner = jnp.where(first, pltpu.roll(t, LANES - HD // 2, 1), pltpu.roll(t, HD // 2, 1))
    return t * cos + partner * sin_signed


def _inproj_kernel(*refs, latent):
    if latent:
        (x_ref, sh_ref, sc_ref, g_ref, w_ref, qg_ref, kg_ref, cos_ref, sin_ref,
         q_ref, k_ref, v_ref, gd_ref, z_ref, ab_ref) = refs
    else:
        (x_ref, sh_ref, sc_ref, g_ref, w_ref, kg_ref,
         k_ref, v_ref, gd_ref, ab_ref) = refs
    x = x_ref[...]
    h = _rms(x, g_ref[...]) * (1.0 + sc_ref[...]) + sh_ref[...]
    y = jnp.dot(h.astype(BF16), w_ref[...], preferred_element_type=F32)

    kt = _head_norm(y[:, COL_K:COL_K + KV_W], kg_ref[...])
    if latent:
        cos = cos_ref[...]
        sin = sin_ref[...]
        kt = _rope(kt, cos, sin)
        for i in range(ATTN_W // LANES):
            qt = y[:, COL_Q + i * LANES:COL_Q + (i + 1) * LANES]
            qt = _rope(_head_norm(qt, qg_ref[...]), cos, sin) * Q_SCALE
            q_ref[:, i * LANES:(i + 1) * LANES] = qt.astype(BF16)
        z_ref[...] = y[:, COL_Z:COL_Z + GDN_W].astype(BF16)
    k_ref[...] = kt.astype(BF16)
    v_ref[...] = y[:, COL_V:COL_V + KV_W].astype(BF16)
    gd_ref[...] = y[:, COL_G:COL_G + GDN_QKV_W].astype(BF16)
    ab_ref[...] = y[:, COL_AB:COL_AB + LANES]


def _inproj_call(x, shift, scale, gain, w_pack, qg, kg, cos, sin, *, latent, tm):
    b, t, d = x.shape
    nt = t // tm
    per_batch = shift.shape[0] == b
    mod_map = (lambda i, j: (i, 0, 0)) if per_batch else (lambda i, j: (0, 0, 0))
    row = lambda i, j: (i, j, 0)
    const2 = lambda i, j: (0, 0)
    in_specs = [pl.BlockSpec((None, tm, d), row),
                pl.BlockSpec((None, 1, d), mod_map),
                pl.BlockSpec((None, 1, d), mod_map),
                pl.BlockSpec((1, d), const2),
                pl.BlockSpec((d, PACK_W), const2)]
    args = [x, shift, scale, gain, w_pack]
    if latent:
        in_specs += [pl.BlockSpec((1, LANES), const2), pl.BlockSpec((1, LANES), const2),
                     pl.BlockSpec((tm, LANES), lambda i, j: (j, 0)),
                     pl.BlockSpec((tm, LANES), lambda i, j: (j, 0))]
        args += [qg, kg, cos, sin]
    else:
        in_specs += [pl.BlockSpec((1, LANES), const2)]
        args += [kg]
    kv_spec = pl.BlockSpec((None, tm, KV_W), row)
    kv_shape = jax.ShapeDtypeStruct((b, t, KV_W), BF16)
    out_specs = [kv_spec, kv_spec, pl.BlockSpec((None, tm, GDN_QKV_W), row)]
    out_shape = [kv_shape, kv_shape, jax.ShapeDtypeStruct((b, t, GDN_QKV_W), BF16)]
    if latent:
        out_specs = [pl.BlockSpec((None, tm, ATTN_W), row)] + out_specs + [pl.BlockSpec((None, tm, GDN_W), row)]
        out_shape = [jax.ShapeDtypeStruct((b, t, ATTN_W), BF16)] + out_shape + [jax.ShapeDtypeStruct((b, t, GDN_W), BF16)]
    out_specs.append(pl.BlockSpec((None, tm, LANES), row))
    out_shape.append(jax.ShapeDtypeStruct((b, t, LANES), F32))
    return pl.pallas_call(
        functools.partial(_inproj_kernel, latent=latent),
        grid=(b, nt),
        in_specs=in_specs,
        out_specs=out_specs,
        out_shape=out_shape,
        compiler_params=_cparams(("parallel", "parallel")),
        name="inproj_lat" if latent else "inproj_ctx",
    )(*args)


def _attn_kernel(q_ref, kc_ref, vc_ref, kl_ref, vl_ref, o_ref, vt_ref, kmax2_ref, *, tq, tk):
    j = pl.program_id(1)
    tc = kc_ref.shape[0]
    n_lat = kl_ref.shape[0]

    @pl.when(pl.program_id(2) == 0)
    def _():
        def put(off, v):
            vt = v.astype(F32).T
            other = (lax.broadcasted_iota(jnp.int32, vt.shape, 0) // HD + j) & 1
            vt_ref[:, off:off + v.shape[0]] = jnp.where(other == 1, 1.0, vt).astype(BF16)

        def key_norm2(k):
            kf = k.astype(F32)
            mine = (lax.broadcasted_iota(jnp.int32, kf.shape, 1) // HD) == j
            n2 = jnp.sum(jnp.where(mine, kf * kf, 0.0), axis=1, keepdims=True)
            return jnp.max(n2, axis=0, keepdims=True)

        put(0, vc_ref[...])
        kmax2 = key_norm2(kc_ref[...])
        for c in range(n_lat // tk):
            put(tc + c * tk, vl_ref[c * tk:(c + 1) * tk, :])
            kmax2 = jnp.maximum(kmax2, key_norm2(kl_ref[c * tk:(c + 1) * tk, :]))
        kmax2_ref[...] = jnp.broadcast_to(kmax2, kmax2_ref.shape)

    qt = q_ref[...].astype(F32).T
    parts = []
    for g in range(GQA):
        qg = qt[g * HD:(g + 1) * HD, :]
        parts.append(jnp.concatenate([jnp.where(j == 0, qg, 0.0), jnp.where(j == 1, qg, 0.0)], axis=0))
    wq = jnp.concatenate(parts, axis=1).astype(BF16)
    nq = wq.shape[1]

    segs = [(kc_ref, 0, 0, tc)] + [(kl_ref, c * tk, tc + c * tk, tk) for c in range(n_lat // tk)]

    def scores(seg):
        k_ref, start, _, size = seg
        return jnp.dot(k_ref[start:start + size, :], wq, preferred_element_type=F32)

    def finish(acc):
        ot = jnp.where(j == 0, acc[:HD], acc[HD:])
        ot = ot / jnp.where(j == 0, acc[HD:HD + 1], acc[0:1])
        og = jnp.concatenate([ot[:, g * tq:(g + 1) * tq] for g in range(GQA)], axis=0)
        o_ref[...] = og.T.astype(BF16)

    wqf = wq.astype(F32)
    bound = jnp.sqrt(jnp.sum(wqf * wqf, axis=0, keepdims=True) * kmax2_ref[0:1, 0:1])
    bounded = jnp.max(bound) <= SOFTMAX_SAFE_BOUND

    @pl.when(bounded)
    def _():
        acc = jnp.zeros((KV_W, nq), F32)
        s_next = scores(segs[0])
        for idx, (_, _, voff, size) in enumerate(segs):
            s = s_next
            if idx + 1 < len(segs):
                s_next = scores(segs[idx + 1])
            p = jnp.exp2(s - bound).astype(BF16)
            acc = acc + jnp.dot(vt_ref[:, voff:voff + size], p, preferred_element_type=F32)
        finish(acc)

    @pl.when(jnp.logical_not(bounded))
    def _():
        m = jnp.full((1, nq), -jnp.inf, F32)
        acc = jnp.zeros((KV_W, nq), F32)
        s_next = scores(segs[0])
        for idx, (_, _, voff, size) in enumerate(segs):
            s = s_next
            if idx + 1 < len(segs):
                s_next = scores(segs[idx + 1])
            m_new = jnp.maximum(m, jnp.max(s, axis=0, keepdims=True))
            alpha = jnp.exp2(m - m_new)
            p = jnp.exp2(s - m_new).astype(BF16)
            acc = alpha * acc + jnp.dot(vt_ref[:, voff:voff + size], p,
                                        preferred_element_type=F32)
            m = m_new
        finish(acc)


def _attn_call(q, kc, vc, kl, vl, *, tq, tk):
    b, t, _ = q.shape
    tc = kc.shape[1]
    gw = GQA * HD
    kv_c = pl.BlockSpec((None, tc, KV_W), lambda i, j, n: (i, 0, 0))
    kv_l = pl.BlockSpec((None, t, KV_W), lambda i, j, n: (i, 0, 0))
    qo = pl.BlockSpec((None, tq, gw), lambda i, j, n: (i, n, j))
    return pl.pallas_call(
        functools.partial(_attn_kernel, tq=tq, tk=tk),
        grid=(b, KV_ATTN, t // tq),
        in_specs=[qo, kv_c, kv_c, kv_l, kv_l],
        out_specs=qo,
        out_shape=jax.ShapeDtypeStruct((b, t, ATTN_W), BF16),
        scratch_shapes=[pltpu.VMEM((KV_W, tc + t), BF16), pltpu.VMEM((SUBLANES, LANES), F32)],
        compiler_params=_cparams(("parallel", "arbitrary", "arbitrary")),
        name="attn",
    )(q, kc, vc, kl, vl)


def _gdn_prep(x_ref, xp_ref, xn_ref, cw_ref, ab_ref, par_ref,
              wq_ref, ik_ref, u_ref, dl_ref, *, tb, t_idx, nt):
    c = CHUNK
    chunk0 = t_idx * (tb // c)

    def rows_of(n):
        return pl.ds(pl.multiple_of((chunk0 + n) * c, c), c)

    x = x_ref[...].astype(F32)
    row = lax.broadcasted_iota(jnp.int32, x.shape, 0)
    prev_row = jnp.where(t_idx > 0, xp_ref[...].astype(F32)[HALO - 1:HALO, :], 0.0)
    next_row = jnp.where(t_idx < nt - 1, xn_ref[...].astype(F32)[0:1, :], 0.0)
    xm = jnp.where(row == 0, prev_row, pltpu.roll(x, 1, 0))
    xp = jnp.where(row == tb - 1, next_row, pltpu.roll(x, tb - 1, 0))
    w = cw_ref[...]

    def l2n(a):
        return a * lax.rsqrt(jnp.sum(a * a, axis=-1, keepdims=True) + EPS)

    ab = ab_ref[...]
    par = par_ref[...]
    g_all = -jnp.exp(par[0:1, :]) * jax.nn.softplus(ab + par[1:2, :])
    beta_all = jax.nn.sigmoid(ab)

    ri = lax.broadcasted_iota(jnp.int32, (c, c), 0)
    ci = lax.broadcasted_iota(jnp.int32, (c, c), 1)
    eye = (ri == ci).astype(F32)
    xr = ri ^ ci
    lane = lax.broadcasted_iota(jnp.int32, (c, LANES), 1)
    tri_f = (ri >= ci).astype(F32)
    tri_b = (ri <= ci).astype(F32)

    def blockdiag(xy):
        z = jnp.zeros((c, c), xy.dtype)
        return jnp.concatenate([jnp.concatenate([xy[:, :c], z], axis=1),
                                jnp.concatenate([z, xy[:, c:]], axis=1)], axis=0)

    def prep_chunk(n, chains):
        sl = slice(n * c, (n + 1) * c)
        y = _silu(xm[sl] * w[0:1, :] + x[sl] * w[1:2, :] + xp[sl] * w[2:3, :])
        qs = [l2n(y[:, h * DK:(h + 1) * DK]) * (DK ** -0.5) for h in range(H_GDN)]
        ks = [l2n(y[:, GDN_QK + h * DK:GDN_QK + (h + 1) * DK]) for h in range(H_GDN)]
        vs = [y[:, 2 * GDN_QK + h * DK:2 * GDN_QK + (h + 1) * DK] for h in range(H_GDN)]
        yield
        g_c = g_all[sl]
        beta_c = beta_all[sl]
        cs_f = jnp.dot(tri_f, g_c, preferred_element_type=F32, precision=lax.Precision.HIGHEST)
        cs_b = jnp.dot(tri_b, g_c, preferred_element_type=F32, precision=lax.Precision.HIGHEST)
        gc_all = jnp.where((lane & H_GDN) == 0, cs_f, cs_b)
        gc_t = gc_all.T
        kkqk_h = []
        for h0 in range(0, H_GDN, 2):
            kq2 = jnp.concatenate(
                [jnp.concatenate([ks[h], qs[h]], axis=0) for h in (h0, h0 + 1)],
                axis=1).astype(BF16)
            k2 = blockdiag(jnp.concatenate([ks[h0], ks[h0 + 1]], axis=1).astype(BF16))
            res = lax.dot_general(kq2, k2, (((1,), (1,)), ((), ())), preferred_element_type=F32)
            kkqk_h += [res[:, :c], res[:, c:]]
        yield
        for h in range(H_GDN):
            qc, kc, vc = qs[h], ks[h], vs[h]
            kk, qk = kkqk_h[h][:c], kkqk_h[h][c:]
            k_t = kc.T
            for d in range(2):
                gi = d * H_GDN + h
                bi = 2 * H_GDN + gi
                gcol = gc_all[:, gi:gi + 1]
                bcol = beta_c[:, bi:bi + 1]
                grow = gc_t[gi:gi + 1, :]
                incl = (ri >= ci) if d == 0 else (ri <= ci)
                strict = (ri > ci) if d == 0 else (ri < ci)
                dec = jnp.exp(jnp.where(incl, gcol - grow, -jnp.inf))
                a_mat = jnp.where(strict, kk * bcol * dec, 0.0)
                egc = jnp.exp(gcol)
                rhs = jnp.concatenate([vc * bcol, kc * (bcol * egc)], axis=1).astype(BF16)
                glast = gcol[c - 1:c, :] if d == 0 else gcol[0:1, :]
                wq_ref[d, h, chunk0 + n, c:2 * c, :] = (qc * egc).astype(BF16)
                ik_ref[d, h, chunk0 + n, 0:c, :] = (qk * dec).astype(BF16)
                ik_ref[d, h, chunk0 + n, c:c + DK, :] = (k_t * jnp.exp(glast - grow)).astype(BF16)
                dl_ref[d, h, chunk0 + n] = jnp.broadcast_to(jnp.exp(glast), (SUBLANES, LANES))
                chains.append((d, h, n, a_mat, rhs))
            yield

    xr2 = jnp.concatenate([xr, xr], axis=1)
    eye2 = jnp.concatenate([eye, eye], axis=1)

    def solve(chains):
        a_pairs = [jnp.concatenate([chains[i][3], chains[i + 1][3]], axis=1)
                   for i in range(0, len(chains), 2)]
        inv_pairs = [eye2 - jnp.where((xr2 >> 1) == 0, ap, 0.0) for ap in a_pairs]
        for lvl in range(1, int(np.log2(c))):
            sel = ((xr2 >> (lvl + 1)) == 0) & ((xr2 >> lvl) != 0)
            invb = [ip.astype(BF16) for ip in inv_pairs]
            mts = [jnp.dot(jnp.where(sel, ap, 0.0).astype(BF16), blockdiag(ib),
                           preferred_element_type=F32) for ap, ib in zip(a_pairs, invb)]
            yield
            inv_pairs = [ip - jnp.dot(ib, blockdiag(mt.astype(BF16)), preferred_element_type=F32)
                         for ip, ib, mt in zip(inv_pairs, invb, mts)]
            yield
        invs = [ip[:, s * c:(s + 1) * c] for ip in inv_pairs for s in range(2)]
        for (d, h, n, _, rhs), inv in zip(chains, invs):
            uw = jnp.dot(inv.astype(BF16), rhs, preferred_element_type=F32)
            u_ref[d, h, rows_of(n), :] = uw[:, :DK].astype(BF16)
            wq_ref[d, h, chunk0 + n, 0:c, :] = uw[:, DK:].astype(BF16)

    def prep_group(chunk_ids, chains):
        for n in chunk_ids:
            yield from prep_chunk(n, chains)

    n_chunks = tb // c
    group = min(GDN_GROUP, n_chunks)
    pending = []
    for _ in prep_group(range(group), pending):
        pass
    for g0 in range(group, n_chunks + group, group):
        current, pending = pending, []
        nxt = prep_group(range(g0, min(g0 + group, n_chunks)), pending)
        for _ in solve(current):
            next(nxt, None)
        for _ in nxt:
            pass


def _gdn_scan(wq_ref, ik_ref, u_ref, dl_ref, s_ref, of_ref, ob_ref, *, nb, j, ns):
    c = CHUNK
    base = (j * nb, (ns - 1 - j) * nb)
    chains = [(d, h) for d in range(2) for h in range(H_GDN)]
    for n in range(nb):
        cns = [n if d == 0 else nb - 1 - n for d, _ in chains]
        ss = [s_ref[d, h] for d, h in chains]
        r1s = [jnp.dot(wq_ref[d, h, base[d] + cn], s.astype(BF16), preferred_element_type=F32)
               for (d, h), cn, s in zip(chains, cns, ss)]
        vns = [u_ref[d, h, pl.ds(pl.multiple_of((base[d] + cn) * c, c), c), :].astype(F32) - r1[:c]
               for (d, h), cn, r1 in zip(chains, cns, r1s)]
        r2s = [jnp.dot(ik_ref[d, h, base[d] + cn], vn.astype(BF16), preferred_element_type=F32)
               for (d, h), cn, vn in zip(chains, cns, vns)]
        for (d, h), cn, s, r1, r2 in zip(chains, cns, ss, r1s, r2s):
            if of_ref is not None:
                o_ref = of_ref if d == 0 else ob_ref
                o_ref[cn * c:(cn + 1) * c, h * DK:(h + 1) * DK] = (r1[c:] + r2[:c]).astype(BF16)
            s_ref[d, h] = s * dl_ref[d, h, base[d] + cn, 0:1, 0:1] + r2[c:]


def _gdn_kernel(x_ref, xp_ref, xn_ref, cw_ref, ab_ref, par_ref, s0_ref, *refs,
                tb, nb, n_prep, n_scan, with_out):
    if with_out:
        of_ref, ob_ref, sfin_ref, wq_s, ik_s, u_s, dl_s, s_ref = refs
    else:
        sfin_ref, wq_s, ik_s, u_s, dl_s, s_ref = refs
        of_ref = ob_ref = None
    step = pl.program_id(1)

    @pl.when(step < n_prep)
    def _():
        _gdn_prep(x_ref, xp_ref, xn_ref, cw_ref, ab_ref, par_ref, wq_s, ik_s, u_s, dl_s,
                  tb=tb, t_idx=step, nt=n_prep)

    @pl.when(step == n_prep)
    def _():
        s_ref[...] = s0_ref[...]

    @pl.when(step >= n_prep)
    def _():
        _gdn_scan(wq_s, ik_s, u_s, dl_s, s_ref, of_ref, ob_ref, nb=nb, j=step - n_prep, ns=n_scan)

    @pl.when(step == n_prep + n_scan - 1)
    def _():
        sfin_ref[...] = s_ref[...]


def _gdn_call(gd, conv_w, ab, par, s0, *, tb, nb, with_out):
    b, t, _ = gd.shape
    c = CHUNK
    nchunk = t // c
    n_prep = t // tb
    n_scan = nchunk // nb
    nb8 = t // HALO
    tb8 = tb // HALO
    blk = lambda j: jnp.minimum(j, n_prep - 1)
    sj = lambda j: jnp.maximum(j - n_prep, 0)
    in_specs = [pl.BlockSpec((None, tb, GDN_QKV_W), lambda i, j: (i, blk(j), 0)),
                pl.BlockSpec((None, HALO, GDN_QKV_W),
                             lambda i, j: (i, jnp.maximum(blk(j) * tb8 - 1, 0), 0)),
                pl.BlockSpec((None, HALO, GDN_QKV_W),
                             lambda i, j: (i, jnp.minimum((blk(j) + 1) * tb8, nb8 - 1), 0)),
                pl.BlockSpec((3, GDN_QKV_W), lambda i, j: (0, 0)),
                pl.BlockSpec((None, tb, LANES), lambda i, j: (i, blk(j), 0)),
                pl.BlockSpec((SUBLANES, LANES), lambda i, j: (0, 0)),
                pl.BlockSpec((None, 2, H_GDN, DK, DK), lambda i, j: (i, 0, 0, 0, 0))]
    out_specs = []
    out_shape = []
    if with_out:
        out_specs += [pl.BlockSpec((None, nb * c, GDN_W), lambda i, j: (i, sj(j), 0)),
                      pl.BlockSpec((None, nb * c, GDN_W), lambda i, j: (i, n_scan - 1 - sj(j), 0))]
        out_shape += [jax.ShapeDtypeStruct((b, t, GDN_W), BF16)] * 2
    out_specs.append(pl.BlockSpec((None, 2, H_GDN, DK, DK), lambda i, j: (i, 0, 0, 0, 0)))
    out_shape.append(jax.ShapeDtypeStruct((b, 2, H_GDN, DK, DK), F32))
    return pl.pallas_call(
        functools.partial(_gdn_kernel, tb=tb, nb=nb, n_prep=n_prep, n_scan=n_scan,
                          with_out=with_out),
        grid=(b, n_prep + n_scan),
        in_specs=in_specs,
        out_specs=out_specs,
        out_shape=out_shape,
        scratch_shapes=[pltpu.VMEM((2, H_GDN, nchunk, 2 * c, DK), BF16),
                        pltpu.VMEM((2, H_GDN, nchunk, c + DK, c), BF16),
                        pltpu.VMEM((2, H_GDN, t, DK), BF16),
                        pltpu.VMEM((2, H_GDN, nchunk, SUBLANES, LANES), F32),
                        pltpu.VMEM((2, H_GDN, DK, DK), F32)],
        compiler_params=_cparams(("parallel", "arbitrary"), GDN_VMEM_LIMIT_BYTES),
        name="gdn" if with_out else "gdn_ctx",
    )(gd, gd, gd, conv_w, ab, par, s0)


def _mix_ffn_kernel(x_ref, xp_ref, xn_ref, oa_ref, oap_ref, oan_ref, of_ref, ofp_ref, ofn_ref,
                    ob_ref, obp_ref, obn_ref, z_ref, zp_ref, zn_ref,
                    g1_ref, ag_ref, gg_ref, wo_ref,
                    sh_ref, sc_ref, g2_ref, ng_ref, wu_ref, cw_ref, cb_ref, wd_ref, fg_ref,
                    o_ref, *, tm, nc):
    t_idx = pl.program_id(1)
    nt = pl.num_programs(1)
    d_ff = wd_ref.shape[0]

    def rows3(a, ap, an):
        return jnp.concatenate([a[...], ap[...], an[...]], axis=0).astype(F32)

    oa = _rms(rows3(oa_ref, oap_ref, oan_ref), ag_ref[...])
    og = rows3(of_ref, ofp_ref, ofn_ref) + rows3(ob_ref, obp_ref, obn_ref)
    z = rows3(z_ref, zp_ref, zn_ref)
    gg = gg_ref[...]
    parts = [oa.astype(BF16)]
    for hd in range(H_GDN):
        sl = slice(hd * DK, (hd + 1) * DK)
        parts.append((_rms(og[:, sl], gg) * _silu(z[:, sl])).astype(BF16))
    lat = jnp.dot(jnp.concatenate(parts, axis=1), wo_ref[...], preferred_element_type=F32)
    xa = rows3(x_ref, xp_ref, xn_ref) + g1_ref[...] * lat
    x = xa[:tm]

    ng, sc, sh = ng_ref[...], sc_ref[...], sh_ref[...]
    h = (_rms(xa, ng) * (1.0 + sc) + sh).astype(BF16)
    row8 = lax.broadcasted_iota(jnp.int32, (SUBLANES, nc), 0)
    has_prev = t_idx > 0
    has_next = t_idx < nt - 1

    def up_proj(c):
        return [jnp.dot(h, wu_ref[:, base:base + nc], preferred_element_type=F32)
                for base in (c * nc, d_ff + c * nc)]

    def conv_act(c, ups):
        us = []
        for base, ua in zip((c * nc, d_ff + c * nc), ups):
            up = ua[:tm]
            prev_row = jnp.where(has_prev, ua[tm + HALO - 1:tm + HALO, :], 0.0)
            next_row = jnp.where(has_next, ua[tm + HALO:tm + HALO + 1, :], 0.0)
            dn = pltpu.roll(up, 1, 0)
            um = jnp.concatenate([jnp.where(row8 == 0, prev_row, dn[:SUBLANES]), dn[SUBLANES:]], axis=0)
            nx = pltpu.roll(up, tm - 1, 0)
            upn = jnp.concatenate([nx[:tm - SUBLANES],
                                   jnp.where(row8 == SUBLANES - 1, next_row, nx[tm - SUBLANES:])], axis=0)
            w = cw_ref[:, base:base + nc]
            us.append(um * w[0:1, :] + up * w[1:2, :] + upn * w[2:3, :] + cb_ref[:, base:base + nc])
        return (_silu(us[0]) * us[1]).astype(BF16)

    n_chunks = d_ff // nc
    acts = []
    ups = up_proj(0)
    for c in range(n_chunks):
        ups_next = up_proj(c + 1) if c + 1 < n_chunks else None
        acts.append(conv_act(c, ups))
        ups = ups_next
    acc = jnp.dot(jnp.concatenate(acts, axis=1), wd_ref[...], preferred_element_type=F32)
    x2 = x + g2_ref[...] * acc
    o_ref[...] = _rms(x2, fg_ref[...])


def _mix_ffn_call(x, oa, of, ob, z, g1, ag, gg, w_out,
                  sh2, sc2, g2, ng, w_up, conv_w, conv_b, w_down, fg, *, tm, nc):
    b, t, d = x.shape
    d_ff = w_down.shape[0]
    nb8 = t // HALO
    tm8 = tm // HALO
    mod = lambda i, j: (i, 0, 0)
    const2 = lambda i, j: (0, 0)

    def with_halo(width):
        return [pl.BlockSpec((None, tm, width), lambda i, j: (i, j, 0)),
                pl.BlockSpec((None, HALO, width),
                             lambda i, j: (i, jnp.maximum(j * tm8 - 1, 0), 0)),
                pl.BlockSpec((None, HALO, width),
                             lambda i, j: (i, jnp.minimum((j + 1) * tm8, nb8 - 1), 0))]

    def resident(shape):
        return pl.BlockSpec(shape, const2, pipeline_mode=pl.Buffered(1))

    in_specs = (with_halo(d) + with_halo(ATTN_W) + with_halo(GDN_W) + with_halo(GDN_W)
                + with_halo(GDN_W)
                + [pl.BlockSpec((None, 1, d), mod),
                   pl.BlockSpec((1, ATTN_W), const2),
                   pl.BlockSpec((1, DK), const2),
                   resident((ATTN_W + GDN_W, d)),
                   pl.BlockSpec((None, 1, d), mod),
                   pl.BlockSpec((None, 1, d), mod),
                   pl.BlockSpec((None, 1, d), mod),
                   pl.BlockSpec((1, d), const2),
                   resident((d, 2 * d_ff)),
                   pl.BlockSpec((3, 2 * d_ff), const2),
                   pl.BlockSpec((1, 2 * d_ff), const2),
                   resident((d_ff, d)),
                   pl.BlockSpec((1, d), const2)])
    return pl.pallas_call(
        functools.partial(_mix_ffn_kernel, tm=tm, nc=nc),
        grid=(b, t // tm),
        in_specs=in_specs,
        out_specs=pl.BlockSpec((None, tm, d), lambda i, j: (i, j, 0)),
        out_shape=jax.ShapeDtypeStruct((b, t, d), F32),
        compiler_params=_cparams(("parallel", "parallel")),
        name="mix_ffn",
    )(x, x, x, oa, oa, oa, of, of, of, ob, ob, ob, z, z, z, g1, ag, gg, w_out,
      sh2, sc2, g2, ng, w_up, conv_w, conv_b, w_down, fg)


def _rope_tables(t):
    rows = t // GRID_W
    r = jnp.repeat(jnp.arange(rows, dtype=F32), GRID_W)
    c = jnp.tile(jnp.arange(GRID_W, dtype=F32), rows)
    pairs = HD // 4
    inv_freq = ROPE_THETA ** (-jnp.arange(pairs, dtype=F32) / pairs)
    ang = jnp.concatenate([r[:, None] * inv_freq, c[:, None] * inv_freq], axis=-1)
    cos, sin = jnp.cos(ang), jnp.sin(ang)
    cos128 = jnp.tile(cos, (1, LANES // (HD // 2)))
    sin128 = jnp.tile(jnp.concatenate([-sin, sin], axis=-1), (1, LANES // HD))
    return cos128, sin128


def _pick(n, candidates):
    for c in candidates:
        if n % c == 0:
            return c
    raise ValueError(f"no tile size in {candidates} divides {n}")


def kernel(x, c, ctx, c_ctx, w_mod, b_mod, norm1_g, w_in, q_norm_g, k_norm_g, attn_out_g,
           conv_qkv_w, a_log_f, a_log_b, dt_bias_f, dt_bias_b, gdn_norm_g, w_out, norm2_g,
           w_up, ffn_conv_w, ffn_conv_b, w_down, final_norm_g):
    depth = w_mod.shape[0]
    assert depth == 1, "only the single-layer configuration is implemented"
    b, t, d = x.shape
    tc = ctx.shape[1]
    assert t % CHUNK == 0 and tc % CHUNK == 0 and t % GRID_W == 0

    rows = -(-(b + 1) // SUBLANES) * SUBLANES
    cc = jnp.zeros((rows, d), F32).at[:b].set(c).at[b].set(c_ctx)
    mod = _mod_call(cc, w_mod[0], b_mod[0]).reshape(rows, N_MOD, d)
    lat = [mod[:b, i][:, None, :] for i in range(N_MOD)]
    cm = [mod[b:b + 1, i][:, None, :] for i in range(2)]

    w_pack = jnp.pad(w_in[0], ((0, 0), (0, PACK_W - w_in.shape[-1]))).astype(BF16)
    qg = jnp.tile(q_norm_g[0], LANES // HD)[None, :]
    kg = jnp.tile(k_norm_g[0], LANES // HD)[None, :]
    cos, sin = _rope_tables(t)
    g1n = norm1_g[0][None, :]

    tm = _pick(t, (512, 256, 128))
    q, kl, vl, gdl, z, abl = _inproj_call(x, lat[0], lat[1], g1n, w_pack, qg, kg, cos, sin,
                                          latent=True, tm=_pick(t, (1024, 512, 256, 128)))
    kc, vc, gdc, abc = _inproj_call(ctx, cm[0], cm[1], g1n, w_pack, None, kg, None, None,
                                    latent=False, tm=_pick(tc, (256, 128)))

    o_attn = _attn_call(q, kc, vc, kl, vl, tq=_pick(t, (1024, 512, 256, 128)), tk=_pick(t, (512, 256, 128)))

    par = jnp.zeros((SUBLANES, LANES), F32)
    par = par.at[0, :2 * H_GDN].set(jnp.concatenate([a_log_f[0], a_log_b[0]]))
    par = par.at[1, :2 * H_GDN].set(jnp.concatenate([dt_bias_f[0], dt_bias_b[0]]))
    cw = conv_qkv_w[0]
    s0 = jnp.zeros((b, 2, H_GDN, DK, DK), F32)
    (s_ctx,) = _gdn_call(gdc, cw, abc, par, s0, tb=_pick(tc, (256, 128)), nb=tc // CHUNK,
                         with_out=False)
    o_f, o_b, _ = _gdn_call(gdl, cw, abl, par, s_ctx, tb=_pick(t, (512, 256, 128)),
                            nb=_pick(t // CHUNK, (8, 4, 2, 1)), with_out=True)

    return _mix_ffn_call(x, o_attn, o_f, o_b, z, lat[2], attn_out_g[0][None, :],
                         gdn_norm_g[0][None, :], w_out[0].astype(BF16),
                         lat[3], lat[4], lat[5], norm2_g[0][None, :], w_up[0].astype(BF16),
                         ffn_conv_w[0], ffn_conv_b[0][None, :], w_down[0].astype(BF16),
                         final_norm_g[None, :], tm=tm, nc=256)
```

```python
import functools

import jax
import jax.numpy as jnp
import numpy as np
from jax import lax
from jax.experimental import pallas as pl
from jax.experimental.pallas import tpu as pltpu

F32 = jnp.float32
BF16 = jnp.bfloat16

GRID_W = 64
H_ATTN = 8
KV_ATTN = 2
HD = 64
GQA = H_ATTN // KV_ATTN
ATTN_W = H_ATTN * HD
KV_W = KV_ATTN * HD
H_GDN = 4
DK = 128
GDN_QK = H_GDN * DK
GDN_QKV_W = 3 * GDN_QK
GDN_W = H_GDN * DK
N_MOD = 6
ROPE_THETA = 10000.0
EPS = 1e-6
Q_SCALE = HD ** -0.5 * float(np.log2(np.e))
SOFTMAX_SAFE_BOUND = 60.0

LANES = 128
SUBLANES = 8
HALO = 16
VMEM_LIMIT_BYTES = 56 * 1024 * 1024
GDN_VMEM_LIMIT_BYTES = 62 * 1024 * 1024

CHUNK = 128
GDN_GROUP = 2

COL_Q = 0
COL_K = COL_Q + ATTN_W
COL_V = COL_K + KV_W
COL_G = COL_V + KV_W
COL_Z = COL_G + GDN_QKV_W
COL_AB = COL_Z + GDN_W
PACK_W = COL_AB + LANES


def _cparams(sem, vmem_limit_bytes=VMEM_LIMIT_BYTES):
    return pltpu.CompilerParams(dimension_semantics=sem, vmem_limit_bytes=vmem_limit_bytes)


def _silu(x):
    return x * jax.nn.sigmoid(x)


def _rms(x, gain):
    return x * lax.rsqrt(jnp.mean(x * x, axis=-1, keepdims=True) + EPS) * gain


def _mod_kernel(c_ref, w_ref, b_ref, o_ref):
    s = _silu(c_ref[...])
    o_ref[...] = jnp.dot(s, w_ref[...], preferred_element_type=F32,
                         precision=lax.Precision.HIGHEST) + b_ref[...]


def _mod_call(cc, w_mod, b_mod):
    rows, d = cc.shape
    n = w_mod.shape[1]
    bn = 1024
    return pl.pallas_call(
        _mod_kernel,
        grid=(n // bn,),
        in_specs=[pl.BlockSpec((rows, d), lambda j: (0, 0)),
                  pl.BlockSpec((d, bn), lambda j: (0, j)),
                  pl.BlockSpec((1, bn), lambda j: (0, j))],
        out_specs=pl.BlockSpec((rows, bn), lambda j: (0, j)),
        out_shape=jax.ShapeDtypeStruct((rows, n), F32),
        compiler_params=_cparams(("arbitrary",)),
        name="mod",
    )(cc, w_mod, b_mod.reshape(1, n))


def _low_half(shape):
    return (lax.broadcasted_iota(jnp.int32, shape, 1) & HD) == 0


def _head_norm(t, gain):
    lo = _low_half(t.shape)
    sq = t * t
    s_lo = jnp.sum(jnp.where(lo, sq, 0.0), axis=-1, keepdims=True)
    s_hi = jnp.sum(jnp.where(lo, 0.0, sq), axis=-1, keepdims=True)
    ms = jnp.where(lo, s_lo, s_hi) * (1.0 / HD)
    return t * lax.rsqrt(ms + EPS) * gain


def _rope(t, cos, sin_signed):
    first = (lax.broadcasted_iota(jnp.int32, t.shape, 1) & (HD // 2)) == 0
    partner = jnp.where(first, pltpu.roll(t, LANES - HD // 2, 1), pltpu.roll(t, HD // 2, 1))
    return t * cos + partner * sin_signed


def _inproj_kernel(*refs, latent):
    if latent:
        (x_ref, sh_ref, sc_ref, g_ref, w_ref, qg_ref, kg_ref, cos_ref, sin_ref,
         q_ref, k_ref, v_ref, gd_ref, z_ref, ab_ref) = refs
    else:
        (x_ref, sh_ref, sc_ref, g_ref, w_ref, kg_ref,
         k_ref, v_ref, gd_ref, ab_ref) = refs
    x = x_ref[...]
    h = _rms(x, g_ref[...]) * (1.0 + sc_ref[...]) + sh_ref[...]
    y = jnp.dot(h.astype(BF16), w_ref[...], preferred_element_type=F32)

    kt = _head_norm(y[:, COL_K:COL_K + KV_W], kg_ref[...])
    if latent:
        cos = cos_ref[...]
        sin = sin_ref[...]
        kt = _rope(kt, cos, sin)
        for i in range(ATTN_W // LANES):
            qt = y[:, COL_Q + i * LANES:COL_Q + (i + 1) * LANES]
            qt = _rope(_head_norm(qt, qg_ref[...]), cos, sin) * Q_SCALE
            q_ref[:, i * LANES:(i + 1) * LANES] = qt.astype(BF16)
        z_ref[...] = y[:, COL_Z:COL_Z + GDN_W].astype(BF16)
    k_ref[...] = kt.astype(BF16)
    v_ref[...] = y[:, COL_V:COL_V + KV_W].astype(BF16)
    gd_ref[...] = y[:, COL_G:COL_G + GDN_QKV_W].astype(BF16)
    ab_ref[...] = y[:, COL_AB:COL_AB + LANES]


def _inproj_call(x, shift, scale, gain, w_pack, qg, kg, cos, sin, *, latent, tm):
    b, t, d = x.shape
    nt = t // tm
    per_batch = shift.shape[0] == b
    mod_map = (lambda i, j: (i, 0, 0)) if per_batch else (lambda i, j: (0, 0, 0))
    row = lambda i, j: (i, j, 0)
    const2 = lambda i, j: (0, 0)
    in_specs = [pl.BlockSpec((None, tm, d), row),
                pl.BlockSpec((None, 1, d), mod_map),
                pl.BlockSpec((None, 1, d), mod_map),
                pl.BlockSpec((1, d), const2),
                pl.BlockSpec((d, PACK_W), const2)]
    args = [x, shift, scale, gain, w_pack]
    if latent:
        in_specs += [pl.BlockSpec((1, LANES), const2), pl.BlockSpec((1, LANES), const2),
                     pl.BlockSpec((tm, LANES), lambda i, j: (j, 0)),
                     pl.BlockSpec((tm, LANES), lambda i, j: (j, 0))]
        args += [qg, kg, cos, sin]
    else:
        in_specs += [pl.BlockSpec((1, LANES), const2)]
        args += [kg]
    kv_spec = pl.BlockSpec((None, tm, KV_W), row)
    kv_shape = jax.ShapeDtypeStruct((b, t, KV_W), BF16)
    out_specs = [kv_spec, kv_spec, pl.BlockSpec((None, tm, GDN_QKV_W), row)]
    out_shape = [kv_shape, kv_shape, jax.ShapeDtypeStruct((b, t, GDN_QKV_W), BF16)]
    if latent:
        out_specs = [pl.BlockSpec((None, tm, ATTN_W), row)] + out_specs + [pl.BlockSpec((None, tm, GDN_W), row)]
        out_shape = [jax.ShapeDtypeStruct((b, t, ATTN_W), BF16)] + out_shape + [jax.ShapeDtypeStruct((b, t, GDN_W), BF16)]
    out_specs.append(pl.BlockSpec((None, tm, LANES), row))
    out_shape.append(jax.ShapeDtypeStruct((b, t, LANES), F32))
    return pl.pallas_call(
        functools.partial(_inproj_kernel, latent=latent),
        grid=(b, nt),
        in_specs=in_specs,
        out_specs=out_specs,
        out_shape=out_shape,
        compiler_params=_cparams(("parallel", "parallel")),
        name="inproj_lat" if latent else "inproj_ctx",
    )(*args)


def _attn_kernel(q_ref, kc_ref, vc_ref, kl_ref, vl_ref, o_ref, vt_ref, kmax2_ref, *, tq, tk):
    j = pl.program_id(1)
    tc = kc_ref.shape[0]
    n_lat = kl_ref.shape[0]

    @pl.when(pl.program_id(2) == 0)
    def _():
        def put(off, v):
            vt = v.astype(F32).T
            other = (lax.broadcasted_iota(jnp.int32, vt.shape, 0) // HD + j) & 1
            vt_ref[:, off:off + v.shape[0]] = jnp.where(other == 1, 1.0, vt).astype(BF16)

        def key_norm2(k):
            kf = k.astype(F32)
            mine = (lax.broadcasted_iota(jnp.int32, kf.shape, 1) // HD) == j
            n2 = jnp.sum(jnp.where(mine, kf * kf, 0.0), axis=1, keepdims=True)
            return jnp.max(n2, axis=0, keepdims=True)

        put(0, vc_ref[...])
        kmax2 = key_norm2(kc_ref[...])
        for c in range(n_lat // tk):
            put(tc + c * tk, vl_ref[c * tk:(c + 1) * tk, :])
            kmax2 = jnp.maximum(kmax2, key_norm2(kl_ref[c * tk:(c + 1) * tk, :]))
        kmax2_ref[...] = jnp.broadcast_to(kmax2, kmax2_ref.shape)

    qt = q_ref[...].astype(F32).T
    parts = []
    for g in range(GQA):
        qg = qt[g * HD:(g + 1) * HD, :]
        parts.append(jnp.concatenate([jnp.where(j == 0, qg, 0.0), jnp.where(j == 1, qg, 0.0)], axis=0))
    wq = jnp.concatenate(parts, axis=1).astype(BF16)
    nq = wq.shape[1]

    segs = [(kc_ref, 0, 0, tc)] + [(kl_ref, c * tk, tc + c * tk, tk) for c in range(n_lat // tk)]

    def scores(seg):
        k_ref, start, _, size = seg
        return jnp.dot(k_ref[start:start + size, :], wq, preferred_element_type=F32)

    def finish(acc):
        ot = jnp.where(j == 0, acc[:HD], acc[HD:])
        ot = ot / jnp.where(j == 0, acc[HD:HD + 1], acc[0:1])
        og = jnp.concatenate([ot[:, g * tq:(g + 1) * tq] for g in range(GQA)], axis=0)
        o_ref[...] = og.T.astype(BF16)

    wqf = wq.astype(F32)
    bound = jnp.sqrt(jnp.sum(wqf * wqf, axis=0, keepdims=True) * kmax2_ref[0:1, 0:1])
    bounded = jnp.max(bound) <= SOFTMAX_SAFE_BOUND

    @pl.when(bounded)
    def _():
        acc = jnp.zeros((KV_W, nq), F32)
        s_next = scores(segs[0])
        for idx, (_, _, voff, size) in enumerate(segs):
            s = s_next
            if idx + 1 < len(segs):
                s_next = scores(segs[idx + 1])
            p = jnp.exp2(s - bound).astype(BF16)
            acc = acc + jnp.dot(vt_ref[:, voff:voff + size], p, preferred_element_type=F32)
        finish(acc)

    @pl.when(jnp.logical_not(bounded))
    def _():
        m = jnp.full((1, nq), -jnp.inf, F32)
        acc = jnp.zeros((KV_W, nq), F32)
        s_next = scores(segs[0])
        for idx, (_, _, voff, size) in enumerate(segs):
            s = s_next
            if idx + 1 < len(segs):
                s_next = scores(segs[idx + 1])
            m_new = jnp.maximum(m, jnp.max(s, axis=0, keepdims=True))
            alpha = jnp.exp2(m - m_new)
            p = jnp.exp2(s - m_new).astype(BF16)
            acc = alpha * acc + jnp.dot(vt_ref[:, voff:voff + size], p,
                                        preferred_element_type=F32)
            m = m_new
        finish(acc)


def _attn_call(q, kc, vc, kl, vl, *, tq, tk):
    b, t, _ = q.shape
    tc = kc.shape[1]
    gw = GQA * HD
    kv_c = pl.BlockSpec((None, tc, KV_W), lambda i, j, n: (i, 0, 0))
    kv_l = pl.BlockSpec((None, t, KV_W), lambda i, j, n: (i, 0, 0))
    qo = pl.BlockSpec((None, tq, gw), lambda i, j, n: (i, n, j))
    return pl.pallas_call(
        functools.partial(_attn_kernel, tq=tq, tk=tk),
        grid=(b, KV_ATTN, t // tq),
        in_specs=[qo, kv_c, kv_c, kv_l, kv_l],
        out_specs=qo,
        out_shape=jax.ShapeDtypeStruct((b, t, ATTN_W), BF16),
        scratch_shapes=[pltpu.VMEM((KV_W, tc + t), BF16), pltpu.VMEM((SUBLANES, LANES), F32)],
        compiler_params=_cparams(("parallel", "arbitrary", "arbitrary")),
        name="attn",
    )(q, kc, vc, kl, vl)


def _gdn_prep(x_ref, xp_ref, xn_ref, cw_ref, ab_ref, par_ref,
              wq_ref, ik_ref, u_ref, dl_ref, *, tb, t_idx, nt):
    c = CHUNK
    chunk0 = t_idx * (tb // c)

    def rows_of(n):
        return pl.ds(pl.multiple_of((chunk0 + n) * c, c), c)

    x = x_ref[...].astype(F32)
    row = lax.broadcasted_iota(jnp.int32, x.shape, 0)
    prev_row = jnp.where(t_idx > 0, xp_ref[...].astype(F32)[HALO - 1:HALO, :], 0.0)
    next_row = jnp.where(t_idx < nt - 1, xn_ref[...].astype(F32)[0:1, :], 0.0)
    xm = jnp.where(row == 0, prev_row, pltpu.roll(x, 1, 0))
    xp = jnp.where(row == tb - 1, next_row, pltpu.roll(x, tb - 1, 0))
    w = cw_ref[...]

    def l2n(a):
        return a * lax.rsqrt(jnp.sum(a * a, axis=-1, keepdims=True) + EPS)

    ab = ab_ref[...]
    par = par_ref[...]
    g_all = -jnp.exp(par[0:1, :]) * jax.nn.softplus(ab + par[1:2, :])
    beta_all = jax.nn.sigmoid(ab)

    ri = lax.broadcasted_iota(jnp.int32, (c, c), 0)
    ci = lax.broadcasted_iota(jnp.int32, (c, c), 1)
    eye = (ri == ci).astype(F32)
    xr = ri ^ ci
    lane = lax.broadcasted_iota(jnp.int32, (c, LANES), 1)
    tri_f = (ri >= ci).astype(F32)
    tri_b = (ri <= ci).astype(F32)

    def blockdiag(xy):
        z = jnp.zeros((c, c), xy.dtype)
        return jnp.concatenate([jnp.concatenate([xy[:, :c], z], axis=1),
                                jnp.concatenate([z, xy[:, c:]], axis=1)], axis=0)

    def prep_chunk(n, chains):
        sl = slice(n * c, (n + 1) * c)
        y = _silu(xm[sl] * w[0:1, :] + x[sl] * w[1:2, :] + xp[sl] * w[2:3, :])
        qs = [l2n(y[:, h * DK:(h + 1) * DK]) * (DK ** -0.5) for h in range(H_GDN)]
        ks = [l2n(y[:, GDN_QK + h * DK:GDN_QK + (h + 1) * DK]) for h in range(H_GDN)]
        vs = [y[:, 2 * GDN_QK + h * DK:2 * GDN_QK + (h + 1) * DK] for h in range(H_GDN)]
        yield
        g_c = g_all[sl]
        beta_c = beta_all[sl]
        cs_f = jnp.dot(tri_f, g_c, preferred_element_type=F32, precision=lax.Precision.HIGHEST)
        cs_b = jnp.dot(tri_b, g_c, preferred_element_type=F32, precision=lax.Precision.HIGHEST)
        gc_all = jnp.where((lane & H_GDN) == 0, cs_f, cs_b)
        gc_t = gc_all.T
        kkqk_h = []
        for h0 in range(0, H_GDN, 2):
            kq2 = jnp.concatenate(
                [jnp.concatenate([ks[h], qs[h]], axis=0) for h in (h0, h0 + 1)],
                axis=1).astype(BF16)
            k2 = blockdiag(jnp.concatenate([ks[h0], ks[h0 + 1]], axis=1).astype(BF16))
            res = lax.dot_general(kq2, k2, (((1,), (1,)), ((), ())), preferred_element_type=F32)
            kkqk_h += [res[:, :c], res[:, c:]]
        yield
        for h in range(H_GDN):
            qc, kc, vc = qs[h], ks[h], vs[h]
            kk, qk = kkqk_h[h][:c], kkqk_h[h][c:]
            k_t = kc.T
            for d in range(2):
                gi = d * H_GDN + h
                bi = 2 * H_GDN + gi
                gcol = gc_all[:, gi:gi + 1]
                bcol = beta_c[:, bi:bi + 1]
                grow = gc_t[gi:gi + 1, :]
                incl = (ri >= ci) if d == 0 else (ri <= ci)
                strict = (ri > ci) if d == 0 else (ri < ci)
                dec = jnp.exp(jnp.where(incl, gcol - grow, -jnp.inf))
                a_mat = jnp.where(strict, kk * bcol * dec, 0.0)
                egc = jnp.exp(gcol)
                rhs = jnp.concatenate([vc * bcol, kc * (bcol * egc)], axis=1).astype(BF16)
                glast = gcol[c - 1:c, :] if d == 0 else gcol[0:1, :]
                wq_ref[d, h, chunk0 + n, c:2 * c, :] = (qc * egc).astype(BF16)
                ik_ref[d, h, chunk0 + n, 0:c, :] = (qk * dec).astype(BF16)
                ik_ref[d, h, chunk0 + n, c:c + DK, :] = (k_t * jnp.exp(glast - grow)).astype(BF16)
                dl_ref[d, h, chunk0 + n] = jnp.broadcast_to(jnp.exp(glast), (SUBLANES, LANES))
                chains.append((d, h, n, a_mat, rhs))
            yield

    xr2 = jnp.concatenate([xr, xr], axis=1)
    eye2 = jnp.concatenate([eye, eye], axis=1)

    def solve(chains):
        a_pairs = [jnp.concatenate([chains[i][3], chains[i + 1][3]], axis=1)
                   for i in range(0, len(chains), 2)]
        inv_pairs = [eye2 - jnp.where((xr2 >> 1) == 0, ap, 0.0) for ap in a_pairs]
        for lvl in range(1, int(np.log2(c))):
            sel = ((xr2 >> (lvl + 1)) == 0) & ((xr2 >> lvl) != 0)
            invb = [ip.astype(BF16) for ip in inv_pairs]
            mts = [jnp.dot(jnp.where(sel, ap, 0.0).astype(BF16), blockdiag(ib),
                           preferred_element_type=F32) for ap, ib in zip(a_pairs, invb)]
            yield
            inv_pairs = [ip - jnp.dot(ib, blockdiag(mt.astype(BF16)), preferred_element_type=F32)
                         for ip, ib, mt in zip(inv_pairs, invb, mts)]
            yield
        invs = [ip[:, s * c:(s + 1) * c] for ip in inv_pairs for s in range(2)]
        for (d, h, n, _, rhs), inv in zip(chains, invs):
            uw = jnp.dot(inv.astype(BF16), rhs, preferred_element_type=F32)
            u_ref[d, h, rows_of(n), :] = uw[:, :DK].astype(BF16)
            wq_ref[d, h, chunk0 + n, 0:c, :] = uw[:, DK:].astype(BF16)

    def prep_group(chunk_ids, chains):
        for n in chunk_ids:
            yield from prep_chunk(n, chains)

    n_chunks = tb // c
    group = min(GDN_GROUP, n_chunks)
    pending = []
    for _ in prep_group(range(group), pending):
        pass
    for g0 in range(group, n_chunks + group, group):
        current, pending = pending, []
        nxt = prep_group(range(g0, min(g0 + group, n_chunks)), pending)
        for _ in solve(current):
            next(nxt, None)
        for _ in nxt:
            pass


def _gdn_scan(wq_ref, ik_ref, u_ref, dl_ref, s_ref, of_ref, ob_ref, *, nb, j, ns):
    c = CHUNK
    base = (j * nb, (ns - 1 - j) * nb)
    chains = [(d, h) for d in range(2) for h in range(H_GDN)]
    for n in range(nb):
        cns = [n if d == 0 else nb - 1 - n for d, _ in chains]
        ss = [s_ref[d, h] for d, h in chains]
        r1s = [jnp.dot(wq_ref[d, h, base[d] + cn], s.astype(BF16), preferred_element_type=F32)
               for (d, h), cn, s in zip(chains, cns, ss)]
        vns = [u_ref[d, h, pl.ds(pl.multiple_of((base[d] + cn) * c, c), c), :].astype(F32) - r1[:c]
               for (d, h), cn, r1 in zip(chains, cns, r1s)]
        r2s = [jnp.dot(ik_ref[d, h, base[d] + cn], vn.astype(BF16), preferred_element_type=F32)
               for (d, h), cn, vn in zip(chains, cns, vns)]
        for (d, h), cn, s, r1, r2 in zip(chains, cns, ss, r1s, r2s):
            if of_ref is not None:
                o_ref = of_ref if d == 0 else ob_ref
                o_ref[cn * c:(cn + 1) * c, h * DK:(h + 1) * DK] = (r1[c:] + r2[:c]).astype(BF16)
            s_ref[d, h] = s * dl_ref[d, h, base[d] + cn, 0:1, 0:1] + r2[c:]


def _gdn_kernel(x_ref, xp_ref, xn_ref, cw_ref, ab_ref, par_ref, s0_ref, *refs,
                tb, nb, n_prep, n_scan, with_out):
    if with_out:
        of_ref, ob_ref, sfin_ref, wq_s, ik_s, u_s, dl_s, s_ref = refs
    else:
        sfin_ref, wq_s, ik_s, u_s, dl_s, s_ref = refs
        of_ref = ob_ref = None
    step = pl.program_id(1)

    @pl.when(step < n_prep)
    def _():
        _gdn_prep(x_ref, xp_ref, xn_ref, cw_ref, ab_ref, par_ref, wq_s, ik_s, u_s, dl_s,
                  tb=tb, t_idx=step, nt=n_prep)

    @pl.when(step == n_prep)
    def _():
        s_ref[...] = s0_ref[...]

    @pl.when(step >= n_prep)
    def _():
        _gdn_scan(wq_s, ik_s, u_s, dl_s, s_ref, of_ref, ob_ref, nb=nb, j=step - n_prep, ns=n_scan)

    @pl.when(step == n_prep + n_scan - 1)
    def _():
        sfin_ref[...] = s_ref[...]


def _gdn_call(gd, conv_w, ab, par, s0, *, tb, nb, with_out):
    b, t, _ = gd.shape
    c = CHUNK
    nchunk = t // c
    n_prep = t // tb
    n_scan = nchunk // nb
    nb8 = t // HALO
    tb8 = tb // HALO
    blk = lambda j: jnp.minimum(j, n_prep - 1)
    sj = lambda j: jnp.maximum(j - n_prep, 0)
    in_specs = [pl.BlockSpec((None, tb, GDN_QKV_W), lambda i, j: (i, blk(j), 0)),
                pl.BlockSpec((None, HALO, GDN_QKV_W),
                             lambda i, j: (i, jnp.maximum(blk(j) * tb8 - 1, 0), 0)),
                pl.BlockSpec((None, HALO, GDN_QKV_W),
                             lambda i, j: (i, jnp.minimum((blk(j) + 1) * tb8, nb8 - 1), 0)),
                pl.BlockSpec((3, GDN_QKV_W), lambda i, j: (0, 0)),
                pl.BlockSpec((None, tb, LANES), lambda i, j: (i, blk(j), 0)),
                pl.BlockSpec((SUBLANES, LANES), lambda i, j: (0, 0)),
                pl.BlockSpec((None, 2, H_GDN, DK, DK), lambda i, j: (i, 0, 0, 0, 0))]
    out_specs = []
    out_shape = []
    if with_out:
        out_specs += [pl.BlockSpec((None, nb * c, GDN_W), lambda i, j: (i, sj(j), 0)),
                      pl.BlockSpec((None, nb * c, GDN_W), lambda i, j: (i, n_scan - 1 - sj(j), 0))]
        out_shape += [jax.ShapeDtypeStruct((b, t, GDN_W), BF16)] * 2
    out_specs.append(pl.BlockSpec((None, 2, H_GDN, DK, DK), lambda i, j: (i, 0, 0, 0, 0)))
    out_shape.append(jax.ShapeDtypeStruct((b, 2, H_GDN, DK, DK), F32))
    return pl.pallas_call(
        functools.partial(_gdn_kernel, tb=tb, nb=nb, n_prep=n_prep, n_scan=n_scan,
                          with_out=with_out),
        grid=(b, n_prep + n_scan),
        in_specs=in_specs,
        out_specs=out_specs,
        out_shape=out_shape,
        scratch_shapes=[pltpu.VMEM((2, H_GDN, nchunk, 2 * c, DK), BF16),
                        pltpu.VMEM((2, H_GDN, nchunk, c + DK, c), BF16),
                        pltpu.VMEM((2, H_GDN, t, DK), BF16),
                        pltpu.VMEM((2, H_GDN, nchunk, SUBLANES, LANES), F32),
                        pltpu.VMEM((2, H_GDN, DK, DK), F32)],
        compiler_params=_cparams(("parallel", "arbitrary"), GDN_VMEM_LIMIT_BYTES),
        name="gdn" if with_out else "gdn_ctx",
    )(gd, gd, gd, conv_w, ab, par, s0)


def _mix_ffn_kernel(x_ref, xp_ref, xn_ref, oa_ref, oap_ref, oan_ref, of_ref, ofp_ref, ofn_ref,
                    ob_ref, obp_ref, obn_ref, z_ref, zp_ref, zn_ref,
                    g1_ref, ag_ref, gg_ref, wo_ref,
                    sh_ref, sc_ref, g2_ref, ng_ref, wu_ref, cw_ref, cb_ref, wd_ref, fg_ref,
                    o_ref, *, tm, nc):
    t_idx = pl.program_id(1)
    nt = pl.num_programs(1)
    d_ff = wd_ref.shape[0]

    def rows3(a, ap, an):
        return jnp.concatenate([a[...].astype(F32), ap[...].astype(F32)[HALO - SUBLANES:],
                                an[...].astype(F32)[:SUBLANES]], axis=0)

    oa = _rms(rows3(oa_ref, oap_ref, oan_ref), ag_ref[...])
    og = rows3(of_ref, ofp_ref, ofn_ref) + rows3(ob_ref, obp_ref, obn_ref)
    z = rows3(z_ref, zp_ref, zn_ref)
    gg = gg_ref[...]
    parts = [oa.astype(BF16)]
    for hd in range(H_GDN):
        sl = slice(hd * DK, (hd + 1) * DK)
        parts.append((_rms(og[:, sl], gg) * _silu(z[:, sl])).astype(BF16))
    lat = jnp.dot(jnp.concatenate(parts, axis=1), wo_ref[...], preferred_element_type=F32)
    xa = rows3(x_ref, xp_ref, xn_ref) + g1_ref[...] * lat
    x = xa[:tm]

    ng, sc, sh = ng_ref[...], sc_ref[...], sh_ref[...]
    h = (_rms(xa, ng) * (1.0 + sc) + sh).astype(BF16)
    row8 = lax.broadcasted_iota(jnp.int32, (SUBLANES, nc), 0)
    has_prev = t_idx > 0
    has_next = t_idx < nt - 1

    def up_proj(c):
        return [jnp.dot(h, wu_ref[:, base:base + nc], preferred_element_type=F32)
                for base in (c * nc, d_ff + c * nc)]

    def conv_act(c, ups):
        us = []
        for base, ua in zip((c * nc, d_ff + c * nc), ups):
            up = ua[:tm]
            prev_row = jnp.where(has_prev, ua[tm + SUBLANES - 1:tm + SUBLANES, :], 0.0)
            next_row = jnp.where(has_next, ua[tm + SUBLANES:tm + SUBLANES + 1, :], 0.0)
            dn = pltpu.roll(up, 1, 0)
            um = jnp.concatenate([jnp.where(row8 == 0, prev_row, dn[:SUBLANES]), dn[SUBLANES:]], axis=0)
            nx = pltpu.roll(up, tm - 1, 0)
            upn = jnp.concatenate([nx[:tm - SUBLANES],
                                   jnp.where(row8 == SUBLANES - 1, next_row, nx[tm - SUBLANES:])], axis=0)
            w = cw_ref[:, base:base + nc]
            us.append(um * w[0:1, :] + up * w[1:2, :] + upn * w[2:3, :] + cb_ref[:, base:base + nc])
        return (_silu(us[0]) * us[1]).astype(BF16)

    n_chunks = d_ff // nc
    acts = []
    ups = up_proj(0)
    for c in range(n_chunks):
        ups_next = up_proj(c + 1) if c + 1 < n_chunks else None
        acts.append(conv_act(c, ups))
        ups = ups_next
    acc = jnp.dot(jnp.concatenate(acts, axis=1), wd_ref[...], preferred_element_type=F32)
    x2 = x + g2_ref[...] * acc
    o_ref[...] = _rms(x2, fg_ref[...])


def _mix_ffn_call(x, oa, of, ob, z, g1, ag, gg, w_out,
                  sh2, sc2, g2, ng, w_up, conv_w, conv_b, w_down, fg, *, tm, nc):
    b, t, d = x.shape
    d_ff = w_down.shape[0]
    nb8 = t // HALO
    tm8 = tm // HALO
    mod = lambda i, j: (i, 0, 0)
    const2 = lambda i, j: (0, 0)

    def with_halo(width):
        return [pl.BlockSpec((None, tm, width), lambda i, j: (i, j, 0)),
                pl.BlockSpec((None, HALO, width),
                             lambda i, j: (i, jnp.maximum(j * tm8 - 1, 0), 0)),
                pl.BlockSpec((None, HALO, width),
                             lambda i, j: (i, jnp.minimum((j + 1) * tm8, nb8 - 1), 0))]

    def resident(shape):
        return pl.BlockSpec(shape, const2, pipeline_mode=pl.Buffered(1))

    in_specs = (with_halo(d) + with_halo(ATTN_W) + with_halo(GDN_W) + with_halo(GDN_W)
                + with_halo(GDN_W)
                + [pl.BlockSpec((None, 1, d), mod),
                   pl.BlockSpec((1, ATTN_W), const2),
                   pl.BlockSpec((1, DK), const2),
                   resident((ATTN_W + GDN_W, d)),
                   pl.BlockSpec((None, 1, d), mod),
                   pl.BlockSpec((None, 1, d), mod),
                   pl.BlockSpec((None, 1, d), mod),
                   pl.BlockSpec((1, d), const2),
                   resident((d, 2 * d_ff)),
                   pl.BlockSpec((3, 2 * d_ff), const2),
                   pl.BlockSpec((1, 2 * d_ff), const2),
                   resident((d_ff, d)),
                   pl.BlockSpec((1, d), const2)])
    return pl.pallas_call(
        functools.partial(_mix_ffn_kernel, tm=tm, nc=nc),
        grid=(b, t // tm),
        in_specs=in_specs,
        out_specs=pl.BlockSpec((None, tm, d), lambda i, j: (i, j, 0)),
        out_shape=jax.ShapeDtypeStruct((b, t, d), F32),
        compiler_params=_cparams(("parallel", "parallel")),
        name="mix_ffn",
    )(x, x, x, oa, oa, oa, of, of, of, ob, ob, ob, z, z, z, g1, ag, gg, w_out,
      sh2, sc2, g2, ng, w_up, conv_w, conv_b, w_down, fg)


def _rope_tables(t):
    rows = t // GRID_W
    r = jnp.repeat(jnp.arange(rows, dtype=F32), GRID_W)
    c = jnp.tile(jnp.arange(GRID_W, dtype=F32), rows)
    pairs = HD // 4
    inv_freq = ROPE_THETA ** (-jnp.arange(pairs, dtype=F32) / pairs)
    ang = jnp.concatenate([r[:, None] * inv_freq, c[:, None] * inv_freq], axis=-1)
    cos, sin = jnp.cos(ang), jnp.sin(ang)
    cos128 = jnp.tile(cos, (1, LANES // (HD // 2)))
    sin128 = jnp.tile(jnp.concatenate([-sin, sin], axis=-1), (1, LANES // HD))
    return cos128, sin128


def _pick(n, candidates):
    for c in candidates:
        if n % c == 0:
            return c
    raise ValueError(f"no tile size in {candidates} divides {n}")


def kernel(x, c, ctx, c_ctx, w_mod, b_mod, norm1_g, w_in, q_norm_g, k_norm_g, attn_out_g,
           conv_qkv_w, a_log_f, a_log_b, dt_bias_f, dt_bias_b, gdn_norm_g, w_out, norm2_g,
           w_up, ffn_conv_w, ffn_conv_b, w_down, final_norm_g):
    depth = w_mod.shape[0]
    assert depth == 1, "only the single-layer configuration is implemented"
    b, t, d = x.shape
    tc = ctx.shape[1]
    assert t % CHUNK == 0 and tc % CHUNK == 0 and t % GRID_W == 0

    rows = -(-(b + 1) // SUBLANES) * SUBLANES
    cc = jnp.zeros((rows, d), F32).at[:b].set(c).at[b].set(c_ctx)
    mod = _mod_call(cc, w_mod[0], b_mod[0]).reshape(rows, N_MOD, d)
    lat = [mod[:b, i][:, None, :] for i in range(N_MOD)]
    cm = [mod[b:b + 1, i][:, None, :] for i in range(2)]

    w_pack = jnp.concatenate(
        [w_in[0].astype(BF16), jnp.zeros((d, PACK_W - w_in.shape[-1]), BF16)], axis=1)
    qg = jnp.tile(q_norm_g[0], LANES // HD)[None, :]
    kg = jnp.tile(k_norm_g[0], LANES // HD)[None, :]
    cos, sin = _rope_tables(t)
    g1n = norm1_g[0][None, :]

    tm = _pick(t, (512, 256, 128))
    q, kl, vl, gdl, z, abl = _inproj_call(x, lat[0], lat[1], g1n, w_pack, qg, kg, cos, sin,
                                          latent=True, tm=_pick(t, (1024, 512, 256, 128)))
    kc, vc, gdc, abc = _inproj_call(ctx, cm[0], cm[1], g1n, w_pack, None, kg, None, None,
                                    latent=False, tm=_pick(tc, (256, 128)))

    o_attn = _attn_call(q, kc, vc, kl, vl, tq=_pick(t, (1024, 512, 256, 128)), tk=_pick(t, (512, 256, 128)))

    par = jnp.zeros((SUBLANES, LANES), F32)
    par = par.at[0, :2 * H_GDN].set(jnp.concatenate([a_log_f[0], a_log_b[0]]))
    par = par.at[1, :2 * H_GDN].set(jnp.concatenate([dt_bias_f[0], dt_bias_b[0]]))
    cw = conv_qkv_w[0]
    s0 = jnp.zeros((b, 2, H_GDN, DK, DK), F32)
    (s_ctx,) = _gdn_call(gdc, cw, abc, par, s0, tb=_pick(tc, (256, 128)), nb=tc // CHUNK,
                         with_out=False)
    o_f, o_b, _ = _gdn_call(gdl, cw, abl, par, s_ctx, tb=_pick(t, (512, 256, 128)),
                            nb=_pick(t // CHUNK, (8, 4, 2, 1)), with_out=True)

    return _mix_ffn_call(x, o_attn, o_f, o_b, z, lat[2], attn_out_g[0][None, :],
                         gdn_norm_g[0][None, :], w_out[0].astype(BF16),
                         lat[3], lat[4], lat[5], norm2_g[0][None, :], w_up[0].astype(BF16),
                         ffn_conv_w[0], ffn_conv_b[0][None, :], w_down[0].astype(BF16),
                         final_norm_g[None, :], tm=tm, nc=256)
```

```python
import functools

import jax
import jax.numpy as jnp
import numpy as np
from jax import lax
from jax.experimental import pallas as pl
from jax.experimental.pallas import tpu as pltpu

F32 = jnp.float32
BF16 = jnp.bfloat16

GRID_W = 64
H_ATTN = 8
KV_ATTN = 2
HD = 64
GQA = H_ATTN // KV_ATTN
ATTN_W = H_ATTN * HD
KV_W = KV_ATTN * HD
H_GDN = 4
DK = 128
GDN_QK = H_GDN * DK
GDN_QKV_W = 3 * GDN_QK
GDN_W = H_GDN * DK
N_MOD = 6
ROPE_THETA = 10000.0
EPS = 1e-6
Q_SCALE = HD ** -0.5 * float(np.log2(np.e))
SOFTMAX_SAFE_BOUND = 60.0

LANES = 128
SUBLANES = 8
HALO = 16
VMEM_LIMIT_BYTES = 56 * 1024 * 1024
GDN_VMEM_LIMIT_BYTES = 62 * 1024 * 1024

CHUNK = 128
GDN_GROUP = 2

COL_Q = 0
COL_K = COL_Q + ATTN_W
COL_V = COL_K + KV_W
COL_G = COL_V + KV_W
COL_Z = COL_G + GDN_QKV_W
COL_AB = COL_Z + GDN_W
PACK_W = COL_AB + LANES


def _cparams(sem, vmem_limit_bytes=VMEM_LIMIT_BYTES):
    return pltpu.CompilerParams(dimension_semantics=sem, vmem_limit_bytes=vmem_limit_bytes)


def _silu(x):
    return x * jax.nn.sigmoid(x)


def _rms(x, gain):
    return x * lax.rsqrt(jnp.mean(x * x, axis=-1, keepdims=True) + EPS) * gain


def _mod_kernel(c_ref, w_ref, b_ref, o_ref):
    s = _silu(c_ref[...])
    o_ref[...] = jnp.dot(s.astype(BF16), w_ref[...].astype(BF16),
                         preferred_element_type=F32) + b_ref[...]


def _mod_call(cc, w_mod, b_mod):
    rows, d = cc.shape
    n = w_mod.shape[1]
    bn = 1024
    return pl.pallas_call(
        _mod_kernel,
        grid=(n // bn,),
        in_specs=[pl.BlockSpec((rows, d), lambda j: (0, 0)),
                  pl.BlockSpec((d, bn), lambda j: (0, j)),
                  pl.BlockSpec((1, bn), lambda j: (0, j))],
        out_specs=pl.BlockSpec((rows, bn), lambda j: (0, j)),
        out_shape=jax.ShapeDtypeStruct((rows, n), F32),
        compiler_params=_cparams(("arbitrary",)),
        name="mod",
    )(cc, w_mod, b_mod.reshape(1, n))


def _low_half(shape):
    return (lax.broadcasted_iota(jnp.int32, shape, 1) & HD) == 0


def _head_norm(t, gain):
    lo = _low_half(t.shape)
    sq = t * t
    s_lo = jnp.sum(jnp.where(lo, sq, 0.0), axis=-1, keepdims=True)
    s_hi = jnp.sum(jnp.where(lo, 0.0, sq), axis=-1, keepdims=True)
    ms = jnp.where(lo, s_lo, s_hi) * (1.0 / HD)
    return t * lax.rsqrt(ms + EPS) * gain


def _rope(t, cos, sin_signed):
    first = (lax.broadcasted_iota(jnp.int32, t.shape, 1) & (HD // 2)) == 0
    partner = jnp.where(first, pltpu.roll(t, LANES - HD // 2, 1), pltpu.roll(t, HD // 2, 1))
    return t * cos + partner * sin_signed


def _inproj_kernel(*refs, latent):
    if latent:
        (x_ref, sh_ref, sc_ref, g_ref, w_ref, qg_ref, kg_ref, cos_ref, sin_ref,
         q_ref, k_ref, v_ref, gd_ref, z_ref, ab_ref) = refs
    else:
        (x_ref, sh_ref, sc_ref, g_ref, w_ref, kg_ref,
         k_ref, v_ref, gd_ref, ab_ref) = refs
    x = x_ref[...]
    h = _rms(x, g_ref[...]) * (1.0 + sc_ref[...]) + sh_ref[...]
    y = jnp.dot(h.astype(BF16), w_ref[...], preferred_element_type=F32)

    kt = _head_norm(y[:, COL_K:COL_K + KV_W], kg_ref[...])
    if latent:
        cos = cos_ref[...]
        sin = sin_ref[...]
        kt = _rope(kt, cos, sin)
        for i in range(ATTN_W // LANES):
            qt = y[:, COL_Q + i * LANES:COL_Q + (i + 1) * LANES]
            qt = _rope(_head_norm(qt, qg_ref[...]), cos, sin) * Q_SCALE
            q_ref[:, i * LANES:(i + 1) * LANES] = qt.astype(BF16)
        z_ref[...] = y[:, COL_Z:COL_Z + GDN_W].astype(BF16)
    k_ref[...] = kt.astype(BF16)
    v_ref[...] = y[:, COL_V:COL_V + KV_W].astype(BF16)
    gd_ref[...] = y[:, COL_G:COL_G + GDN_QKV_W].astype(BF16)
    ab_ref[...] = y[:, COL_AB:COL_AB + LANES]


def _inproj_call(x, shift, scale, gain, w_pack, qg, kg, cos, sin, *, latent, tm):
    b, t, d = x.shape
    nt = t // tm
    per_batch = shift.shape[0] == b
    mod_map = (lambda i, j: (i, 0, 0)) if per_batch else (lambda i, j: (0, 0, 0))
    row = lambda i, j: (i, j, 0)
    const2 = lambda i, j: (0, 0)
    in_specs = [pl.BlockSpec((None, tm, d), row),
                pl.BlockSpec((None, 1, d), mod_map),
                pl.BlockSpec((None, 1, d), mod_map),
                pl.BlockSpec((1, d), const2),
                pl.BlockSpec((d, PACK_W), const2)]
    args = [x, shift, scale, gain, w_pack]
    if latent:
        in_specs += [pl.BlockSpec((1, LANES), const2), pl.BlockSpec((1, LANES), const2),
                     pl.BlockSpec((tm, LANES), lambda i, j: (j, 0)),
                     pl.BlockSpec((tm, LANES), lambda i, j: (j, 0))]
        args += [qg, kg, cos, sin]
    else:
        in_specs += [pl.BlockSpec((1, LANES), const2)]
        args += [kg]
    kv_spec = pl.BlockSpec((None, tm, KV_W), row)
    kv_shape = jax.ShapeDtypeStruct((b, t, KV_W), BF16)
    out_specs = [kv_spec, kv_spec, pl.BlockSpec((None, tm, GDN_QKV_W), row)]
    out_shape = [kv_shape, kv_shape, jax.ShapeDtypeStruct((b, t, GDN_QKV_W), BF16)]
    if latent:
        out_specs = [pl.BlockSpec((None, tm, ATTN_W), row)] + out_specs + [pl.BlockSpec((None, tm, GDN_W), row)]
        out_shape = [jax.ShapeDtypeStruct((b, t, ATTN_W), BF16)] + out_shape + [jax.ShapeDtypeStruct((b, t, GDN_W), BF16)]
    out_specs.append(pl.BlockSpec((None, tm, LANES), row))
    out_shape.append(jax.ShapeDtypeStruct((b, t, LANES), F32))
    return pl.pallas_call(
        functools.partial(_inproj_kernel, latent=latent),
        grid=(b, nt),
        in_specs=in_specs,
        out_specs=out_specs,
        out_shape=out_shape,
        compiler_params=_cparams(("parallel", "parallel")),
        name="inproj_lat" if latent else "inproj_ctx",
    )(*args)


def _attn_kernel(q_ref, kc_ref, vc_ref, kl_ref, vl_ref, o_ref, vt_ref, kmax2_ref, *, tq, tk):
    j = pl.program_id(1)
    tc = kc_ref.shape[0]
    n_lat = kl_ref.shape[0]

    @pl.when(pl.program_id(2) == 0)
    def _():
        def put(off, v):
            vt = v.astype(F32).T
            other = (lax.broadcasted_iota(jnp.int32, vt.shape, 0) // HD + j) & 1
            vt_ref[:, off:off + v.shape[0]] = jnp.where(other == 1, 1.0, vt).astype(BF16)

        def key_norm2(k):
            kf = k.astype(F32)
            mine = (lax.broadcasted_iota(jnp.int32, kf.shape, 1) // HD) == j
            n2 = jnp.sum(jnp.where(mine, kf * kf, 0.0), axis=1, keepdims=True)
            return jnp.max(n2, axis=0, keepdims=True)

        put(0, vc_ref[...])
        kmax2 = key_norm2(kc_ref[...])
        for c in range(n_lat // tk):
            put(tc + c * tk, vl_ref[c * tk:(c + 1) * tk, :])
            kmax2 = jnp.maximum(kmax2, key_norm2(kl_ref[c * tk:(c + 1) * tk, :]))
        kmax2_ref[...] = jnp.broadcast_to(kmax2, kmax2_ref.shape)

    qt = q_ref[...].astype(F32).T
    parts = []
    for g in range(GQA):
        qg = qt[g * HD:(g + 1) * HD, :]
        parts.append(jnp.concatenate([jnp.where(j == 0, qg, 0.0), jnp.where(j == 1, qg, 0.0)], axis=0))
    wq = jnp.concatenate(parts, axis=1).astype(BF16)
    nq = wq.shape[1]

    segs = [(kc_ref, 0, 0, tc)] + [(kl_ref, c * tk, tc + c * tk, tk) for c in range(n_lat // tk)]

    def scores(seg):
        k_ref, start, _, size = seg
        return jnp.dot(k_ref[start:start + size, :], wq, preferred_element_type=F32)

    def finish(acc):
        ot = jnp.where(j == 0, acc[:HD], acc[HD:])
        ot = ot / jnp.where(j == 0, acc[HD:HD + 1], acc[0:1])
        og = jnp.concatenate([ot[:, g * tq:(g + 1) * tq] for g in range(GQA)], axis=0)
        o_ref[...] = og.T.astype(BF16)

    wqf = wq.astype(F32)
    bound = jnp.sqrt(jnp.sum(wqf * wqf, axis=0, keepdims=True) * kmax2_ref[0:1, 0:1])
    bounded = jnp.max(bound) <= SOFTMAX_SAFE_BOUND

    @pl.when(bounded)
    def _():
        acc = jnp.zeros((KV_W, nq), F32)
        s_next = scores(segs[0])
        for idx, (_, _, voff, size) in enumerate(segs):
            s = s_next
            if idx + 1 < len(segs):
                s_next = scores(segs[idx + 1])
            p = jnp.exp2(s - bound).astype(BF16)
            acc = acc + jnp.dot(vt_ref[:, voff:voff + size], p, preferred_element_type=F32)
        finish(acc)

    @pl.when(jnp.logical_not(bounded))
    def _():
        m = jnp.full((1, nq), -jnp.inf, F32)
        acc = jnp.zeros((KV_W, nq), F32)
        s_next = scores(segs[0])
        for idx, (_, _, voff, size) in enumerate(segs):
            s = s_next
            if idx + 1 < len(segs):
                s_next = scores(segs[idx + 1])
            m_new = jnp.maximum(m, jnp.max(s, axis=0, keepdims=True))
            alpha = jnp.exp2(m - m_new)
            p = jnp.exp2(s - m_new).astype(BF16)
            acc = alpha * acc + jnp.dot(vt_ref[:, voff:voff + size], p,
                                        preferred_element_type=F32)
            m = m_new
        finish(acc)


def _attn_call(q, kc, vc, kl, vl, *, tq, tk):
    b, t, _ = q.shape
    tc = kc.shape[1]
    gw = GQA * HD
    kv_c = pl.BlockSpec((None, tc, KV_W), lambda i, j, n: (i, 0, 0))
    kv_l = pl.BlockSpec((None, t, KV_W), lambda i, j, n: (i, 0, 0))
    qo = pl.BlockSpec((None, tq, gw), lambda i, j, n: (i, n, j))
    return pl.pallas_call(
        functools.partial(_attn_kernel, tq=tq, tk=tk),
        grid=(b, KV_ATTN, t // tq),
        in_specs=[qo, kv_c, kv_c, kv_l, kv_l],
        out_specs=qo,
        out_shape=jax.ShapeDtypeStruct((b, t, ATTN_W), BF16),
        scratch_shapes=[pltpu.VMEM((KV_W, tc + t), BF16), pltpu.VMEM((SUBLANES, LANES), F32)],
        compiler_params=_cparams(("parallel", "arbitrary", "arbitrary")),
        name="attn",
    )(q, kc, vc, kl, vl)


def _gdn_prep(x_ref, xp_ref, xn_ref, cw_ref, ab_ref, par_ref,
              wq_ref, ik_ref, u_ref, dl_ref, *, tb, t_idx, nt):
    c = CHUNK
    chunk0 = t_idx * (tb // c)

    def rows_of(n):
        return pl.ds(pl.multiple_of((chunk0 + n) * c, c), c)

    x = x_ref[...].astype(F32)
    row = lax.broadcasted_iota(jnp.int32, x.shape, 0)
    prev_row = jnp.where(t_idx > 0, xp_ref[...].astype(F32)[HALO - 1:HALO, :], 0.0)
    next_row = jnp.where(t_idx < nt - 1, xn_ref[...].astype(F32)[0:1, :], 0.0)
    xm = jnp.where(row == 0, prev_row, pltpu.roll(x, 1, 0))
    xp = jnp.where(row == tb - 1, next_row, pltpu.roll(x, tb - 1, 0))
    w = cw_ref[...]

    def l2n(a):
        return a * lax.rsqrt(jnp.sum(a * a, axis=-1, keepdims=True) + EPS)

    ab = ab_ref[...]
    par = par_ref[...]
    g_all = -jnp.exp(par[0:1, :]) * jax.nn.softplus(ab + par[1:2, :])
    beta_all = jax.nn.sigmoid(ab)

    ri = lax.broadcasted_iota(jnp.int32, (c, c), 0)
    ci = lax.broadcasted_iota(jnp.int32, (c, c), 1)
    eye = (ri == ci).astype(F32)
    xr = ri ^ ci
    lane = lax.broadcasted_iota(jnp.int32, (c, LANES), 1)
    tri_f = (ri >= ci).astype(F32)
    tri_b = (ri <= ci).astype(F32)

    def blockdiag(xy):
        z = jnp.zeros((c, c), xy.dtype)
        return jnp.concatenate([jnp.concatenate([xy[:, :c], z], axis=1),
                                jnp.concatenate([z, xy[:, c:]], axis=1)], axis=0)

    def prep_chunk(n, chains):
        sl = slice(n * c, (n + 1) * c)
        y = _silu(xm[sl] * w[0:1, :] + x[sl] * w[1:2, :] + xp[sl] * w[2:3, :])
        qs = [l2n(y[:, h * DK:(h + 1) * DK]) * (DK ** -0.5) for h in range(H_GDN)]
        ks = [l2n(y[:, GDN_QK + h * DK:GDN_QK + (h + 1) * DK]) for h in range(H_GDN)]
        vs = [y[:, 2 * GDN_QK + h * DK:2 * GDN_QK + (h + 1) * DK] for h in range(H_GDN)]
        yield
        g_c = g_all[sl]
        beta_c = beta_all[sl]
        cs_f = jnp.dot(tri_f, g_c, preferred_element_type=F32, precision=lax.Precision.HIGHEST)
        cs_b = jnp.dot(tri_b, g_c, preferred_element_type=F32, precision=lax.Precision.HIGHEST)
        gc_all = jnp.where((lane & H_GDN) == 0, cs_f, cs_b)
        gc_t = gc_all.T
        kkqk_h = []
        for h0 in range(0, H_GDN, 2):
            kq2 = jnp.concatenate(
                [jnp.concatenate([ks[h], qs[h]], axis=0) for h in (h0, h0 + 1)],
                axis=1).astype(BF16)
            k2 = blockdiag(jnp.concatenate([ks[h0], ks[h0 + 1]], axis=1).astype(BF16))
            res = lax.dot_general(kq2, k2, (((1,), (1,)), ((), ())), preferred_element_type=F32)
            kkqk_h += [res[:, :c], res[:, c:]]
        yield
        for h in range(H_GDN):
            qc, kc, vc = qs[h], ks[h], vs[h]
            kk, qk = kkqk_h[h][:c], kkqk_h[h][c:]
            k_t = kc.T
            for d in range(2):
                gi = d * H_GDN + h
                bi = 2 * H_GDN + gi
                gcol = gc_all[:, gi:gi + 1]
                bcol = beta_c[:, bi:bi + 1]
                grow = gc_t[gi:gi + 1, :]
                incl = (ri >= ci) if d == 0 else (ri <= ci)
                strict = (ri > ci) if d == 0 else (ri < ci)
                dec = jnp.exp(jnp.where(incl, gcol - grow, -jnp.inf))
                a_mat = jnp.where(strict, kk * bcol * dec, 0.0)
                egc = jnp.exp(gcol)
                rhs = jnp.concatenate([vc * bcol, kc * (bcol * egc)], axis=1).astype(BF16)
                glast = gcol[c - 1:c, :] if d == 0 else gcol[0:1, :]
                wq_ref[d, h, chunk0 + n, c:2 * c, :] = (qc * egc).astype(BF16)
                ik_ref[d, h, chunk0 + n, 0:c, :] = (qk * dec).astype(BF16)
                ik_ref[d, h, chunk0 + n, c:c + DK, :] = (k_t * jnp.exp(glast - grow)).astype(BF16)
                dl_ref[d, h, chunk0 + n] = jnp.broadcast_to(jnp.exp(glast), (SUBLANES, LANES))
                chains.append((d, h, n, a_mat, rhs))
            yield

    xr2 = jnp.concatenate([xr, xr], axis=1)
    eye2 = jnp.concatenate([eye, eye], axis=1)

    def solve(chains):
        a_pairs = [jnp.concatenate([chains[i][3], chains[i + 1][3]], axis=1)
                   for i in range(0, len(chains), 2)]
        inv_pairs = [eye2 - jnp.where((xr2 >> 1) == 0, ap, 0.0) for ap in a_pairs]
        for lvl in range(1, int(np.log2(c))):
            sel = ((xr2 >> (lvl + 1)) == 0) & ((xr2 >> lvl) != 0)
            invb = [ip.astype(BF16) for ip in inv_pairs]
            mts = [jnp.dot(jnp.where(sel, ap, 0.0).astype(BF16), blockdiag(ib),
                           preferred_element_type=F32) for ap, ib in zip(a_pairs, invb)]
            yield
            inv_pairs = [ip - jnp.dot(ib, blockdiag(mt.astype(BF16)), preferred_element_type=F32)
                         for ip, ib, mt in zip(inv_pairs, invb, mts)]
            yield
        invs = [ip[:, s * c:(s + 1) * c] for ip in inv_pairs for s in range(2)]
        for (d, h, n, _, rhs), inv in zip(chains, invs):
            uw = jnp.dot(inv.astype(BF16), rhs, preferred_element_type=F32)
            u_ref[d, h, rows_of(n), :] = uw[:, :DK].astype(BF16)
            wq_ref[d, h, chunk0 + n, 0:c, :] = uw[:, DK:].astype(BF16)

    def prep_group(chunk_ids, chains):
        for n in chunk_ids:
            yield from prep_chunk(n, chains)

    n_chunks = tb // c
    group = min(GDN_GROUP, n_chunks)
    pending = []
    for _ in prep_group(range(group), pending):
        pass
    for g0 in range(group, n_chunks + group, group):
        current, pending = pending, []
        nxt = prep_group(range(g0, min(g0 + group, n_chunks)), pending)
        for _ in solve(current):
            next(nxt, None)
        for _ in nxt:
            pass


def _gdn_scan(wq_ref, ik_ref, u_ref, dl_ref, s_ref, of_ref, ob_ref, *, nb, j, ns):
    c = CHUNK
    base = (j * nb, (ns - 1 - j) * nb)
    chains = [(d, h) for d in range(2) for h in range(H_GDN)]
    for n in range(nb):
        cns = [n if d == 0 else nb - 1 - n for d, _ in chains]
        ss = [s_ref[d, h] for d, h in chains]
        r1s = [jnp.dot(wq_ref[d, h, base[d] + cn], s.astype(BF16), preferred_element_type=F32)
               for (d, h), cn, s in zip(chains, cns, ss)]
        vns = [u_ref[d, h, pl.ds(pl.multiple_of((base[d] + cn) * c, c), c), :].astype(F32) - r1[:c]
               for (d, h), cn, r1 in zip(chains, cns, r1s)]
        r2s = [jnp.dot(ik_ref[d, h, base[d] + cn], vn.astype(BF16), preferred_element_type=F32)
               for (d, h), cn, vn in zip(chains, cns, vns)]
        for (d, h), cn, s, r1, r2 in zip(chains, cns, ss, r1s, r2s):
            if of_ref is not None:
                o_ref = of_ref if d == 0 else ob_ref
                o_ref[cn * c:(cn + 1) * c, h * DK:(h + 1) * DK] = (r1[c:] + r2[:c]).astype(BF16)
            s_ref[d, h] = s * dl_ref[d, h, base[d] + cn, 0:1, 0:1] + r2[c:]


def _gdn_kernel(x_ref, xp_ref, xn_ref, cw_ref, ab_ref, par_ref, s0_ref, *refs,
                tb, nb, n_prep, n_scan, with_out):
    if with_out:
        of_ref, ob_ref, sfin_ref, wq_s, ik_s, u_s, dl_s, s_ref = refs
    else:
        sfin_ref, wq_s, ik_s, u_s, dl_s, s_ref = refs
        of_ref = ob_ref = None
    step = pl.program_id(1)

    @pl.when(step < n_prep)
    def _():
        _gdn_prep(x_ref, xp_ref, xn_ref, cw_ref, ab_ref, par_ref, wq_s, ik_s, u_s, dl_s,
                  tb=tb, t_idx=step, nt=n_prep)

    @pl.when(step == n_prep)
    def _():
        s_ref[...] = s0_ref[...]

    @pl.when(step >= n_prep)
    def _():
        _gdn_scan(wq_s, ik_s, u_s, dl_s, s_ref, of_ref, ob_ref, nb=nb, j=step - n_prep, ns=n_scan)

    @pl.when(step == n_prep + n_scan - 1)
    def _():
        sfin_ref[...] = s_ref[...]


def _gdn_call(gd, conv_w, ab, par, s0, *, tb, nb, with_out):
    b, t, _ = gd.shape
    c = CHUNK
    nchunk = t // c
    n_prep = t // tb
    n_scan = nchunk // nb
    nb8 = t // HALO
    tb8 = tb // HALO
    blk = lambda j: jnp.minimum(j, n_prep - 1)
    sj = lambda j: jnp.maximum(j - n_prep, 0)
    in_specs = [pl.BlockSpec((None, tb, GDN_QKV_W), lambda i, j: (i, blk(j), 0)),
                pl.BlockSpec((None, HALO, GDN_QKV_W),
                             lambda i, j: (i, jnp.maximum(blk(j) * tb8 - 1, 0), 0)),
                pl.BlockSpec((None, HALO, GDN_QKV_W),
                             lambda i, j: (i, jnp.minimum((blk(j) + 1) * tb8, nb8 - 1), 0)),
                pl.BlockSpec((3, GDN_QKV_W), lambda i, j: (0, 0)),
                pl.BlockSpec((None, tb, LANES), lambda i, j: (i, blk(j), 0)),
                pl.BlockSpec((SUBLANES, LANES), lambda i, j: (0, 0)),
                pl.BlockSpec((None, 2, H_GDN, DK, DK), lambda i, j: (i, 0, 0, 0, 0))]
    out_specs = []
    out_shape = []
    if with_out:
        out_specs += [pl.BlockSpec((None, nb * c, GDN_W), lambda i, j: (i, sj(j), 0)),
                      pl.BlockSpec((None, nb * c, GDN_W), lambda i, j: (i, n_scan - 1 - sj(j), 0))]
        out_shape += [jax.ShapeDtypeStruct((b, t, GDN_W), BF16)] * 2
    out_specs.append(pl.BlockSpec((None, 2, H_GDN, DK, DK), lambda i, j: (i, 0, 0, 0, 0)))
    out_shape.append(jax.ShapeDtypeStruct((b, 2, H_GDN, DK, DK), F32))
    return pl.pallas_call(
        functools.partial(_gdn_kernel, tb=tb, nb=nb, n_prep=n_prep, n_scan=n_scan,
                          with_out=with_out),
        grid=(b, n_prep + n_scan),
        in_specs=in_specs,
        out_specs=out_specs,
        out_shape=out_shape,
        scratch_shapes=[pltpu.VMEM((2, H_GDN, nchunk, 2 * c, DK), BF16),
                        pltpu.VMEM((2, H_GDN, nchunk, c + DK, c), BF16),
                        pltpu.VMEM((2, H_GDN, t, DK), BF16),
                        pltpu.VMEM((2, H_GDN, nchunk, SUBLANES, LANES), F32),
                        pltpu.VMEM((2, H_GDN, DK, DK), F32)],
        compiler_params=_cparams(("parallel", "arbitrary"), GDN_VMEM_LIMIT_BYTES),
        name="gdn" if with_out else "gdn_ctx",
    )(gd, gd, gd, conv_w, ab, par, s0)


def _mix_ffn_kernel(x_ref, xp_ref, xn_ref, oa_ref, oap_ref, oan_ref, of_ref, ofp_ref, ofn_ref,
                    ob_ref, obp_ref, obn_ref, z_ref, zp_ref, zn_ref,
                    g1_ref, ag_ref, gg_ref, wo_ref,
                    sh_ref, sc_ref, g2_ref, ng_ref, wu_ref, cw_ref, cb_ref, wd_ref, fg_ref,
                    o_ref, *, tm, nc):
    t_idx = pl.program_id(1)
    nt = pl.num_programs(1)
    d_ff = wd_ref.shape[0]

    def rows3(a, ap, an):
        return jnp.concatenate([a[...], ap[...], an[...]], axis=0).astype(F32)

    oa = _rms(rows3(oa_ref, oap_ref, oan_ref), ag_ref[...])
    og = rows3(of_ref, ofp_ref, ofn_ref) + rows3(ob_ref, obp_ref, obn_ref)
    z = rows3(z_ref, zp_ref, zn_ref)
    gg = gg_ref[...]
    parts = [oa.astype(BF16)]
    for hd in range(H_GDN):
        sl = slice(hd * DK, (hd + 1) * DK)
        parts.append((_rms(og[:, sl], gg) * _silu(z[:, sl])).astype(BF16))
    lat = jnp.dot(jnp.concatenate(parts, axis=1), wo_ref[...], preferred_element_type=F32)
    xa = rows3(x_ref, xp_ref, xn_ref) + g1_ref[...] * lat
    x = xa[:tm]

    ng, sc, sh = ng_ref[...], sc_ref[...], sh_ref[...]
    h = (_rms(xa, ng) * (1.0 + sc) + sh).astype(BF16)
    row8 = lax.broadcasted_iota(jnp.int32, (SUBLANES, nc), 0)
    has_prev = t_idx > 0
    has_next = t_idx < nt - 1

    def up_proj(c):
        return [jnp.dot(h, wu_ref[:, base:base + nc], preferred_element_type=F32)
                for base in (c * nc, d_ff + c * nc)]

    def conv_act(c, ups):
        us = []
        for base, ua in zip((c * nc, d_ff + c * nc), ups):
            up = ua[:tm]
            prev_row = jnp.where(has_prev, ua[tm + HALO - 1:tm + HALO, :], 0.0)
            next_row = jnp.where(has_next, ua[tm + HALO:tm + HALO + 1, :], 0.0)
            dn = pltpu.roll(up, 1, 0)
            um = jnp.concatenate([jnp.where(row8 == 0, prev_row, dn[:SUBLANES]), dn[SUBLANES:]], axis=0)
            nx = pltpu.roll(up, tm - 1, 0)
            upn = jnp.concatenate([nx[:tm - SUBLANES],
                                   jnp.where(row8 == SUBLANES - 1, next_row, nx[tm - SUBLANES:])], axis=0)
            w = cw_ref[:, base:base + nc]
            us.append(um * w[0:1, :] + up * w[1:2, :] + upn * w[2:3, :] + cb_ref[:, base:base + nc])
        return (_silu(us[0]) * us[1]).astype(BF16)

    n_chunks = d_ff // nc
    acts = []
    ups = up_proj(0)
    for c in range(n_chunks):
        ups_next = up_proj(c + 1) if c + 1 < n_chunks else None
        acts.append(conv_act(c, ups))
        ups = ups_next
    acc = jnp.dot(jnp.concatenate(acts, axis=1), wd_ref[...], preferred_element_type=F32)
    x2 = x + g2_ref[...] * acc
    o_ref[...] = _rms(x2, fg_ref[...])


def _mix_ffn_call(x, oa, of, ob, z, g1, ag, gg, w_out,
                  sh2, sc2, g2, ng, w_up, conv_w, conv_b, w_down, fg, *, tm, nc):
    b, t, d = x.shape
    d_ff = w_down.shape[0]
    nb8 = t // HALO
    tm8 = tm // HALO
    mod = lambda i, j: (i, 0, 0)
    const2 = lambda i, j: (0, 0)

    def with_halo(width):
        return [pl.BlockSpec((None, tm, width), lambda i, j: (i, j, 0)),
                pl.BlockSpec((None, HALO, width),
                             lambda i, j: (i, jnp.maximum(j * tm8 - 1, 0), 0)),
                pl.BlockSpec((None, HALO, width),
                             lambda i, j: (i, jnp.minimum((j + 1) * tm8, nb8 - 1), 0))]

    def resident(shape):
        return pl.BlockSpec(shape, const2, pipeline_mode=pl.Buffered(1))

    in_specs = (with_halo(d) + with_halo(ATTN_W) + with_halo(GDN_W) + with_halo(GDN_W)
                + with_halo(GDN_W)
                + [pl.BlockSpec((None, 1, d), mod),
                   pl.BlockSpec((1, ATTN_W), const2),
                   pl.BlockSpec((1, DK), const2),
                   resident((ATTN_W + GDN_W, d)),
                   pl.BlockSpec((None, 1, d), mod),
                   pl.BlockSpec((None, 1, d), mod),
                   pl.BlockSpec((None, 1, d), mod),
                   pl.BlockSpec((1, d), const2),
                   resident((d, 2 * d_ff)),
                   pl.BlockSpec((3, 2 * d_ff), const2),
                   pl.BlockSpec((1, 2 * d_ff), const2),
                   resident((d_ff, d)),
                   pl.BlockSpec((1, d), const2)])
    return pl.pallas_call(
        functools.partial(_mix_ffn_kernel, tm=tm, nc=nc),
        grid=(b, t // tm),
        in_specs=in_specs,
        out_specs=pl.BlockSpec((None, tm, d), lambda i, j: (i, j, 0)),
        out_shape=jax.ShapeDtypeStruct((b, t, d), F32),
        compiler_params=_cparams(("parallel", "parallel")),
        name="mix_ffn",
    )(x, x, x, oa, oa, oa, of, of, of, ob, ob, ob, z, z, z, g1, ag, gg, w_out,
      sh2, sc2, g2, ng, w_up, conv_w, conv_b, w_down, fg)


def _rope_tables(t):
    rows = t // GRID_W
    r = jnp.repeat(jnp.arange(rows, dtype=F32), GRID_W)
    c = jnp.tile(jnp.arange(GRID_W, dtype=F32), rows)
    pairs = HD // 4
    inv_freq = ROPE_THETA ** (-jnp.arange(pairs, dtype=F32) / pairs)
    ang = jnp.concatenate([r[:, None] * inv_freq, c[:, None] * inv_freq], axis=-1)
    cos, sin = jnp.cos(ang), jnp.sin(ang)
    cos128 = jnp.tile(cos, (1, LANES // (HD // 2)))
    sin128 = jnp.tile(jnp.concatenate([-sin, sin], axis=-1), (1, LANES // HD))
    return cos128, sin128


def _pick(n, candidates):
    for c in candidates:
        if n % c == 0:
            return c
    raise ValueError(f"no tile size in {candidates} divides {n}")


def kernel(x, c, ctx, c_ctx, w_mod, b_mod, norm1_g, w_in, q_norm_g, k_norm_g, attn_out_g,
           conv_qkv_w, a_log_f, a_log_b, dt_bias_f, dt_bias_b, gdn_norm_g, w_out, norm2_g,
           w_up, ffn_conv_w, ffn_conv_b, w_down, final_norm_g):
    depth = w_mod.shape[0]
    assert depth == 1, "only the single-layer configuration is implemented"
    b, t, d = x.shape
    tc = ctx.shape[1]
    assert t % CHUNK == 0 and tc % CHUNK == 0 and t % GRID_W == 0

    rows = -(-(b + 1) // SUBLANES) * SUBLANES
    cc = jnp.zeros((rows, d), F32).at[:b].set(c).at[b].set(c_ctx)
    mod = _mod_call(cc, w_mod[0], b_mod[0]).reshape(rows, N_MOD, d)
    lat = [mod[:b, i][:, None, :] for i in range(N_MOD)]
    cm = [mod[b:b + 1, i][:, None, :] for i in range(2)]

    w_pack = jnp.pad(w_in[0], ((0, 0), (0, PACK_W - w_in.shape[-1]))).astype(BF16)
    qg = jnp.tile(q_norm_g[0], LANES // HD)[None, :]
    kg = jnp.tile(k_norm_g[0], LANES // HD)[None, :]
    cos, sin = _rope_tables(t)
    g1n = norm1_g[0][None, :]

    tm = _pick(t, (512, 256, 128))
    q, kl, vl, gdl, z, abl = _inproj_call(x, lat[0], lat[1], g1n, w_pack, qg, kg, cos, sin,
                                          latent=True, tm=_pick(t, (1024, 512, 256, 128)))
    kc, vc, gdc, abc = _inproj_call(ctx, cm[0], cm[1], g1n, w_pack, None, kg, None, None,
                                    latent=False, tm=_pick(tc, (256, 128)))

    o_attn = _attn_call(q, kc, vc, kl, vl, tq=_pick(t, (1024, 512, 256, 128)), tk=_pick(t, (512, 256, 128)))

    par = jnp.zeros((SUBLANES, LANES), F32)
    par = par.at[0, :2 * H_GDN].set(jnp.concatenate([a_log_f[0], a_log_b[0]]))
    par = par.at[1, :2 * H_GDN].set(jnp.concatenate([dt_bias_f[0], dt_bias_b[0]]))
    cw = conv_qkv_w[0]
    s0 = jnp.zeros((b, 2, H_GDN, DK, DK), F32)
    (s_ctx,) = _gdn_call(gdc, cw, abc, par, s0, tb=_pick(tc, (256, 128)), nb=tc // CHUNK,
                         with_out=False)
    o_f, o_b, _ = _gdn_call(gdl, cw, abl, par, s_ctx, tb=_pick(t, (512, 256, 128)),
                            nb=_pick(t // CHUNK, (8, 4, 2, 1)), with_out=True)

    return _mix_ffn_call(x, o_attn, o_f, o_b, z, lat[2], attn_out_g[0][None, :],
                         gdn_norm_g[0][None, :], w_out[0].astype(BF16),
                         lat[3], lat[4], lat[5], norm2_g[0][None, :], w_up[0].astype(BF16),
                         ffn_conv_w[0], ffn_conv_b[0][None, :], w_down[0].astype(BF16),
                         final_norm_g[None, :], tm=tm, nc=256)
```
